```python
import jax, jax.numpy as jnp
from jax import lax
import numpy as np

D_MODEL = 4096
BATCH = 4
SEQ = 4096
DEPTH = 1

ATT_HEAD_DIM = 128
ATT_Q_HEADS = 32
ATT_KV_HEADS = 8
ATT_GROUP = ATT_Q_HEADS // ATT_KV_HEADS
ATT_Q_DIM = ATT_Q_HEADS * ATT_HEAD_DIM
ATT_KV_DIM = ATT_KV_HEADS * ATT_HEAD_DIM
ATT_QKV_DIM = ATT_Q_DIM + 2 * ATT_KV_DIM
WINDOW = 128
BLOCK = 128
ROPE_THETA = 10000.0
NEG_INF = -1e30

RWKV_HEAD = 64
RWKV_DIM = D_MODEL
RWKV_HEADS = RWKV_DIM // RWKV_HEAD
D_DECAY = max(32, int(round(1.8 * RWKV_DIM ** 0.5 / 32)) * 32)
D_AAA = max(32, int(round(1.8 * RWKV_DIM ** 0.5 / 32)) * 32)
D_GATE = max(32, int(round(0.6 * RWKV_DIM ** 0.8 / 32)) * 32)
RWKV_SHIFT_DIM = 3 * RWKV_DIM + D_DECAY + D_AAA + D_GATE
RWKV_SPLITS = [RWKV_DIM, 2 * RWKV_DIM, 3 * RWKV_DIM, 3 * RWKV_DIM + D_DECAY, 3 * RWKV_DIM + D_DECAY + D_AAA]
GN_EPS = 64e-5

IN_DIM = ATT_QKV_DIM + RWKV_SHIFT_DIM + 2 * D_MODEL

FFN_DIM = ((8 * D_MODEL + 3 * 256 - 1) // (3 * 256)) * 256
RMS_EPS = 1e-6

kernel_name = 'hybrid_swa_sinks_rwkv7_gated_block'


def _rmsnorm(x, g):
    xf = x.astype(jnp.float32)
    y = xf * lax.rsqrt(jnp.mean(xf * xf, axis=-1, keepdims=True) + RMS_EPS)
    return (y * g.astype(jnp.float32)).astype(x.dtype)


def _rope_tables(seq, dtype):
    pos = jnp.arange(seq, dtype=jnp.float32)
    inv_freq = ROPE_THETA ** (-jnp.arange(0, ATT_HEAD_DIM, 2, dtype=jnp.float32) / ATT_HEAD_DIM)
    ang = pos[:, None] * inv_freq[None, :]
    return jnp.cos(ang).astype(dtype), jnp.sin(ang).astype(dtype)


def _rope(t, cos, sin):
    t1, t2 = jnp.split(t, 2, axis=-1)
    c = cos[None, :, None, :]
    s = sin[None, :, None, :]
    return jnp.concatenate([t1 * c - t2 * s, t2 * c + t1 * s], axis=-1)


def _swa_sinks(q, k, v, sinks):
    B, S = q.shape[0], q.shape[1]
    nb = S // BLOCK
    qb = q.reshape(B, nb, BLOCK, ATT_KV_HEADS, ATT_GROUP, ATT_HEAD_DIM)

    def band(t):
        tb = t.reshape(B, nb, BLOCK, ATT_KV_HEADS, ATT_HEAD_DIM)
        prev = jnp.pad(tb[:, :-1], ((0, 0), (1, 0), (0, 0), (0, 0), (0, 0)))
        return jnp.concatenate([prev, tb], axis=2)

    kb, vb = band(k), band(v)
    scores = jnp.einsum('bnqhgd,bnkhd->bnhgqk', qb, kb,
                        preferred_element_type=jnp.float32) * (ATT_HEAD_DIM ** -0.5)
    qi = jnp.arange(BLOCK)[:, None]
    kj = jnp.arange(2 * BLOCK)[None, :]
    rel = qi + BLOCK - kj
    key_pos = jnp.arange(nb)[:, None, None] * BLOCK + kj[None] - BLOCK
    mask = (rel >= 0) & (rel < WINDOW) & (key_pos >= 0)
    scores = jnp.where(mask[None, :, None, None], scores, NEG_INF)
    sink = sinks.astype(jnp.float32).reshape(ATT_KV_HEADS, ATT_GROUP)[None, None, :, :, None, None]
    m = jnp.maximum(jnp.max(scores, axis=-1, keepdims=True), sink)
    p = jnp.exp(scores - m)
    probs = p / (jnp.sum(p, axis=-1, keepdims=True) + jnp.exp(sink - m))
    o = jnp.einsum('bnhgqk,bnkhd->bnqhgd', probs.astype(v.dtype), vb)
    return o.reshape(B, S, ATT_Q_DIM)


def _wkv7_step(state, inp):
    r, w, k, v, a, b = inp
    sa = jnp.einsum('bhvk,bhk->bhv', state, a)
    state = state * w[:, :, None, :] + sa[..., None] * b[:, :, None, :] + v[..., None] * k[:, :, None, :]
    y = jnp.einsum('bhvk,bhk->bhv', state, r)
    return state, y


def _rwkv7(p_r, p_k, p_v, p_w, p_a, p_g, w0, w2, a0, a2, g2, k_k, k_a, r_k, ln_w, ln_b):
    B, S, C = p_r.shape
    H, N = RWKV_HEADS, RWKV_HEAD
    f32 = jnp.float32
    heads = lambda t: t.astype(f32).reshape(B, S, H, N)
    w = -jax.nn.softplus(-(w0 + jnp.tanh(p_w) @ w2)) - 0.5
    a = jax.nn.sigmoid(a0 + p_a @ a2)
    g = jax.nn.sigmoid(p_g) @ g2
    kk = heads(p_k * k_k)
    kk = kk / jnp.maximum(jnp.sqrt(jnp.sum(kk * kk, axis=-1, keepdims=True)), 1e-12)
    k = heads(p_k * (1.0 + (a - 1.0) * k_a))
    r = heads(p_r)
    v = heads(p_v)
    decay = jnp.exp(-jnp.exp(heads(w)))
    a_vec = -kk
    b_vec = kk * heads(a)
    tm = lambda t: jnp.moveaxis(t, 1, 0)
    state0 = jnp.zeros((B, H, N, N), f32)
    _, y = lax.scan(_wkv7_step, state0, (tm(r), tm(decay), tm(k), tm(v), tm(a_vec), tm(b_vec)))
    y = jnp.moveaxis(y, 0, 1)
    mu = jnp.mean(y, axis=-1, keepdims=True)
    var = jnp.mean(jnp.square(y - mu), axis=-1, keepdims=True)
    y = ((y - mu) * lax.rsqrt(var + GN_EPS)).reshape(B, S, C)
    y = y * ln_w.astype(f32) + ln_b.astype(f32)
    bonus = (jnp.sum(r * k * r_k.astype(f32), axis=-1, keepdims=True) * v).reshape(B, S, C)
    return ((y + bonus) * g.astype(f32)).astype(p_r.dtype)


def _layer(x, norm_mix_pre, norm_mix_post, norm_ffn_pre, norm_ffn_post, w_in, b_qkv, att_sinks,
           mu_shift, w0, w2, a0, a2, g2, k_k, k_a, r_k, ln_x_w, ln_x_b,
           w_att_branch, w_rwkv_branch, w_out, w_ffn_gate, w_ffn_up, w_ffn_down):
    B, S, _ = x.shape
    h = _rmsnorm(x, norm_mix_pre)
    proj = h @ w_in
    att_cols, rwkv_cols, gate_cols = jnp.split(
        proj, [ATT_QKV_DIM, ATT_QKV_DIM + RWKV_SHIFT_DIM], axis=-1)

    att_cols = att_cols + b_qkv
    q, k, v = jnp.split(att_cols, [ATT_Q_DIM, ATT_Q_DIM + ATT_KV_DIM], axis=-1)
    q = q.reshape(B, S, ATT_Q_HEADS, ATT_HEAD_DIM)
    k = k.reshape(B, S, ATT_KV_HEADS, ATT_HEAD_DIM)
    v = v.reshape(B, S, ATT_KV_HEADS, ATT_HEAD_DIM)
    cos, sin = _rope_tables(S, q.dtype)
    o_att = _swa_sinks(_rope(q, cos, sin), _rope(k, cos, sin), v, att_sinks)

    prev = jnp.pad(rwkv_cols[:, :-1], ((0, 0), (1, 0), (0, 0)))
    rwkv_cols = rwkv_cols + (prev - rwkv_cols) * mu_shift
    p_r, p_k, p_v, p_w, p_a, p_g = jnp.split(rwkv_cols, RWKV_SPLITS, axis=-1)
    o_rwkv = _rwkv7(p_r, p_k, p_v, p_w, p_a, p_g, w0, w2, a0, a2, g2, k_k, k_a, r_k, ln_x_w, ln_x_b)

    g_att, g_rwkv = jnp.split(gate_cols, 2, axis=-1)
    merged = (jax.nn.sigmoid(g_att) * (o_att @ w_att_branch)
              + jax.nn.sigmoid(g_rwkv) * (o_rwkv @ w_rwkv_branch))
    x = x + _rmsnorm(merged @ w_out, norm_mix_post)

    h = _rmsnorm(x, norm_ffn_pre)
    f = (jax.nn.silu(h @ w_ffn_gate) * (h @ w_ffn_up)) @ w_ffn_down
    return x + _rmsnorm(f, norm_ffn_post)


def _normal(k, shape, scale):
    return jax.random.normal(k, shape, jnp.float32) * scale


def setup_inputs(seed: int = 0) -> dict:
    key = jax.random.key(seed)
    ks = jax.random.split(key, 26)
    L = DEPTH
    gain = lambda k: 1.0 + _normal(k, (L, D_MODEL), 0.02)
    return {
        'x': _normal(ks[0], (BATCH, SEQ, D_MODEL), 1.0),
        'norm_mix_pre': gain(ks[1]),
        'norm_mix_post': gain(ks[2]),
        'norm_ffn_pre': gain(ks[3]),
        'norm_ffn_post': gain(ks[4]),
        'w_in': _normal(ks[5], (L, D_MODEL, IN_DIM), D_MODEL ** -0.5),
        'b_qkv': _normal(ks[6], (L, ATT_QKV_DIM), 0.02),
        'att_sinks': _normal(ks[7], (L, ATT_Q_HEADS), 0.5),
        'mu_shift': jax.random.uniform(ks[8], (L, RWKV_SHIFT_DIM), jnp.float32, 0.0, 1.0),
        'w0': jax.random.uniform(ks[9], (L, RWKV_DIM), jnp.float32, -3.0, 0.0),
        'w2': _normal(ks[10], (L, D_DECAY, RWKV_DIM), 0.1 * D_DECAY ** -0.5),
        'a0': _normal(ks[11], (L, RWKV_DIM), 0.1),
        'a2': _normal(ks[12], (L, D_AAA, RWKV_DIM), 0.5 * D_AAA ** -0.5),
        'g2': _normal(ks[13], (L, D_GATE, RWKV_DIM), D_GATE ** -0.5),
        'k_k': 0.85 + _normal(ks[14], (L, RWKV_DIM), 0.02),
        'k_a': 1.0 + _normal(ks[15], (L, RWKV_DIM), 0.02),
        'r_k': _normal(ks[16], (L, RWKV_HEADS, RWKV_HEAD), 0.1),
        'ln_x_w': 1.0 + _normal(ks[17], (L, RWKV_DIM), 0.02),
        'ln_x_b': _normal(ks[18], (L, RWKV_DIM), 0.02),
        'w_att_branch': _normal(ks[19], (L, ATT_Q_DIM, D_MODEL), ATT_Q_DIM ** -0.5),
        'w_rwkv_branch': _normal(ks[20], (L, RWKV_DIM, D_MODEL), RWKV_DIM ** -0.5),
        'w_out': _normal(ks[21], (L, D_MODEL, D_MODEL), D_MODEL ** -0.5),
        'w_ffn_gate': _normal(ks[22], (L, D_MODEL, FFN_DIM), D_MODEL ** -0.5),
        'w_ffn_up': _normal(ks[23], (L, D_MODEL, FFN_DIM), D_MODEL ** -0.5),
        'w_ffn_down': _normal(ks[24], (L, FFN_DIM, D_MODEL), FFN_DIM ** -0.5),
    }


def reference(x, norm_mix_pre, norm_mix_post, norm_ffn_pre, norm_ffn_post, w_in, b_qkv, att_sinks,
              mu_shift, w0, w2, a0, a2, g2, k_k, k_a, r_k, ln_x_w, ln_x_b,
              w_att_branch, w_rwkv_branch, w_out, w_ffn_gate, w_ffn_up, w_ffn_down):
    for l in range(DEPTH):
        x = _layer(x, norm_mix_pre[l], norm_mix_post[l], norm_ffn_pre[l], norm_ffn_post[l],
                   w_in[l], b_qkv[l], att_sinks[l], mu_shift[l], w0[l], w2[l], a0[l], a2[l],
                   g2[l], k_k[l], k_a[l], r_k[l], ln_x_w[l], ln_x_b[l],
                   w_att_branch[l], w_rwkv_branch[l], w_out[l],
                   w_ffn_gate[l], w_ffn_up[l], w_ffn_down[l])
    return x
```

```python
import functools

import jax
import jax.numpy as jnp
from jax import lax
from jax.experimental import pallas as pl
from jax.experimental.pallas import tpu as pltpu

F32 = jnp.float32
BF16 = jnp.bfloat16

LANES = 128
SUBLANES = 8
V7X_VMEM_BYTES = 64 * 1024 * 1024
VMEM_CAP = V7X_VMEM_BYTES - 6 * 1024 * 1024

ATT_HEAD = 128
ATT_GROUP = 4
WINDOW = 128
ROPE_THETA = 10000.0
NEG_INF = -1e30
RWKV_HEAD = 64
CHUNK = 64
PAIR = 2 * RWKV_HEAD
GN_EPS = 64e-5
RMS_EPS = 1e-6


def _vmem_limit(block_bytes):
    return int(min(VMEM_CAP, 2 * block_bytes + 16 * 1024 * 1024))


def _nbytes(shape, dtype):
    n = 1
    for s in shape:
        n *= s
    return n * jnp.dtype(dtype).itemsize


def _rms(x, g):
    return x * lax.rsqrt(jnp.mean(x * x, axis=-1, keepdims=True) + RMS_EPS) * g


def _norm_kernel(*refs, has_res, emit_main, second_norm):
    it = iter(refs)
    x_ref, g_ref = next(it), next(it)
    res_ref = next(it) if has_res else None
    g2_ref = next(it) if second_norm else None
    main_ref = next(it) if emit_main else None
    next_ref = next(it) if (second_norm or not emit_main) else None
    y = _rms(x_ref[...], g_ref[...])
    if has_res:
        y = res_ref[...] + y
    if emit_main:
        main_ref[...] = y
    if second_norm:
        next_ref[...] = _rms(y, g2_ref[...]).astype(next_ref.dtype)
    elif not emit_main:
        next_ref[...] = y.astype(next_ref.dtype)


def _norm_call(x, g, res=None, g_next=None, *, emit_main=True, tm=256):
    m, d = x.shape
    tm = min(tm, m)
    second_norm = g_next is not None
    assert emit_main or not second_norm
    row = pl.BlockSpec((tm, d), lambda i: (i, 0))
    vec = pl.BlockSpec((1, d), lambda i: (0, 0))
    args, specs = [x, g.reshape(1, d)], [row, vec]
    if res is not None:
        args.append(res); specs.append(row)
    if second_norm:
        args.append(g_next.reshape(1, d)); specs.append(vec)
    out_shape, out_specs = [], []
    if emit_main:
        out_shape.append(jax.ShapeDtypeStruct((m, d), F32)); out_specs.append(row)
    if second_norm or not emit_main:
        out_shape.append(jax.ShapeDtypeStruct((m, d), BF16)); out_specs.append(row)
    blk = _nbytes((tm, d), F32) * (len(args) + len(out_shape))
    outs = pl.pallas_call(
        functools.partial(_norm_kernel, has_res=res is not None, emit_main=emit_main,
                          second_norm=second_norm),
        grid=(m // tm,),
        in_specs=specs, out_specs=out_specs, out_shape=out_shape,
        compiler_params=pltpu.CompilerParams(
            dimension_semantics=("parallel",), vmem_limit_bytes=_vmem_limit(blk)),
        name="rmsnorm",
    )(*args)
    return outs


def _mm_kernel(*refs, nk, has_bias, act, has_mul, has_add):
    it = iter(refs)
    a_ref, w_ref = next(it), next(it)
    bias_ref = next(it) if has_bias else None
    mul_ref = next(it) if has_mul else None
    add_ref = next(it) if has_add else None
    o_ref = next(it)
    acc_ref = next(it) if nk > 1 else None

    part = jnp.dot(a_ref[...], w_ref[...], preferred_element_type=F32)

    def finish(y):
        if has_bias:
            y = y + bias_ref[...]
        if act == "sigmoid":
            y = jax.nn.sigmoid(y)
        elif act == "swiglu":
            half = y.shape[1] // 2
            gate, up = y[:, :half], y[:, half:]
            y = gate * jax.nn.sigmoid(gate) * up
        if has_mul:
            y = y * mul_ref[...].astype(F32)
        if has_add:
            y = y + add_ref[...].astype(F32)
        o_ref[...] = y.astype(o_ref.dtype)

    if nk == 1:
        finish(part)
    else:
        k = pl.program_id(2)

        @pl.when(k == 0)
        def _():
            acc_ref[...] = part

        @pl.when(jnp.logical_and(k > 0, k < nk - 1))
        def _():
            acc_ref[...] += part

        @pl.when(k == nk - 1)
        def _():
            finish(acc_ref[...] + part)


def _matmul(a, w, *, out_dtype, tm=1024, tn=1024, tk=None, bias=None, act=None,
            mul=None, mul_col0=0, add=None):
    m, kdim = a.shape
    n_w = w.shape[1]
    tm = min(tm, m)
    tk = kdim if tk is None else tk
    nk = kdim // tk
    assert m % tm == 0 and kdim % tk == 0
    tn_w = tn * 2 if act == "swiglu" else tn
    assert n_w % tn_w == 0 and tn % LANES == 0
    n_out = n_w // 2 if act == "swiglu" else n_w
    grid = (m // tm, n_w // tn_w, nk)

    args = [a, w]
    specs = [pl.BlockSpec((tm, tk), lambda i, j, k: (i, k)),
             pl.BlockSpec((tk, tn_w), lambda i, j, k: (k, j))]
    blk = _nbytes((tm, tk), a.dtype) + _nbytes((tk, tn_w), w.dtype)
    if bias is not None:
        args.append(bias.reshape(1, n_w).astype(F32))
        specs.append(pl.BlockSpec((1, tn_w), lambda i, j, k: (0, j)))
    if mul is not None:
        assert mul_col0 % tn == 0
        off = mul_col0 // tn
        args.append(mul)
        specs.append(pl.BlockSpec((tm, tn), lambda i, j, k: (i, j + off)))
        blk += _nbytes((tm, tn), mul.dtype)
    if add is not None:
        args.append(add)
        specs.append(pl.BlockSpec((tm, tn), lambda i, j, k: (i, j)))
        blk += _nbytes((tm, tn), add.dtype)
    blk += _nbytes((tm, tn), out_dtype)
    scratch = [pltpu.VMEM((tm, tn_w), F32)] if nk > 1 else []
    extra = _nbytes((tm, tn_w), F32) * (4 if nk > 1 else 3)

    return pl.pallas_call(
        functools.partial(_mm_kernel, nk=nk, has_bias=bias is not None, act=act,
                          has_mul=mul is not None, has_add=add is not None),
        grid=grid,
        in_specs=specs,
        out_specs=pl.BlockSpec((tm, tn), lambda i, j, k: (i, j)),
        out_shape=jax.ShapeDtypeStruct((m, n_out), out_dtype),
        scratch_shapes=scratch,
        compiler_params=pltpu.CompilerParams(
            dimension_semantics=("parallel", "parallel", "arbitrary"),
            vmem_limit_bytes=int(min(VMEM_CAP, 2 * blk + extra + 4 * 1024 * 1024))),
        name="matmul_" + (act or "plain"),
    )(*args)


def _attn_kernel(sinks_ref, q_ref, kc_ref, kp_ref, vc_ref, vp_ref,
                 cosc_ref, sinc_ref, cosp_ref, sinp_ref, o_ref, *, nb, n_kv):
    i = pl.program_id(0)
    first = (i % nb) == 0
    blk = WINDOW
    scale = ATT_HEAD ** -0.5

    def rope(t, cos, sin):
        t = t.astype(F32)
        return t * cos + pltpu.roll(t, ATT_HEAD // 2, axis=1) * sin

    cos_c, sin_c = cosc_ref[...], sinc_ref[...]
    cos_p, sin_p = cosp_ref[...], sinp_ref[...]

    rows = ATT_GROUP * blk
    qi = lax.broadcasted_iota(jnp.int32, (rows, 2 * blk), 0) & (blk - 1)
    kj = lax.broadcasted_iota(jnp.int32, (rows, 2 * blk), 1)
    lo = jnp.where(first, blk, 0)
    valid = (kj > qi) & (kj <= qi + blk) & (kj >= lo)
    grp = lax.broadcasted_iota(jnp.int32, (rows, 1), 0) >> (blk.bit_length() - 1)

    for h in range(n_kv):
        cs = slice(h * ATT_HEAD, (h + 1) * ATT_HEAD)
        k_band = jnp.concatenate(
            [rope(kp_ref[:, cs], cos_p, sin_p), rope(kc_ref[:, cs], cos_c, sin_c)],
            axis=0).astype(BF16)
        v_band = jnp.concatenate([vp_ref[:, cs], vc_ref[:, cs]], axis=0)
        q_parts, sink = [], jnp.zeros((rows, 1), F32)
        for g in range(ATT_GROUP):
            hq = h * ATT_GROUP + g
            qs = slice(hq * ATT_HEAD, (hq + 1) * ATT_HEAD)
            q_parts.append((rope(q_ref[:, qs], cos_c, sin_c) * scale).astype(BF16))
            sink = jnp.where(grp == g, sinks_ref[hq], sink)
        q_all = jnp.concatenate(q_parts, axis=0)
        s = lax.dot_general(q_all, k_band, (((1,), (1,)), ((), ())),
                            preferred_element_type=F32)
        s = jnp.where(valid, s, NEG_INF)
        m = jnp.maximum(jnp.max(s, axis=-1, keepdims=True), sink)
        p = jnp.exp(s - m)
        denom = jnp.sum(p, axis=-1, keepdims=True) + jnp.exp(sink - m)
        o = jnp.dot(p.astype(BF16), v_band, preferred_element_type=F32) / denom
        for g in range(ATT_GROUP):
            hq = h * ATT_GROUP + g
            o_ref[:, hq * ATT_HEAD:(hq + 1) * ATT_HEAD] = (
                o[g * blk:(g + 1) * blk].astype(o_ref.dtype))


def _attention(qkv, sinks, cos, sin, *, seq, n_q, n_kv):
    m = qkv.shape[0]
    nb = seq // WINDOW
    q_w, kv_w = n_q * ATT_HEAD, n_kv * ATT_HEAD
    k_blk0, v_blk0 = q_w // kv_w, q_w // kv_w + 1
    prev = lambda i: jnp.maximum(i - 1, 0)
    tab_c = pl.BlockSpec((WINDOW, ATT_HEAD), lambda i: (i % nb, 0))
    tab_p = pl.BlockSpec((WINDOW, ATT_HEAD), lambda i: (jnp.maximum(i % nb - 1, 0), 0))
    blk = (_nbytes((WINDOW, q_w), BF16) * 2 + 4 * _nbytes((WINDOW, kv_w), BF16)
           + 4 * _nbytes((WINDOW, ATT_HEAD), F32))
    return pl.pallas_call(
        functools.partial(_attn_kernel, nb=nb, n_kv=n_kv),
        grid=(m // WINDOW,),
        in_specs=[
            pl.BlockSpec(memory_space=pltpu.SMEM),
            pl.BlockSpec((WINDOW, q_w), lambda i: (i, 0)),
            pl.BlockSpec((WINDOW, kv_w), lambda i: (i, k_blk0)),
            pl.BlockSpec((WINDOW, kv_w), lambda i: (prev(i), k_blk0)),
            pl.BlockSpec((WINDOW, kv_w), lambda i: (i, v_blk0)),
            pl.BlockSpec((WINDOW, kv_w), lambda i: (prev(i), v_blk0)),
            tab_c, tab_c, tab_p, tab_p,
        ],
        out_specs=pl.BlockSpec((WINDOW, q_w), lambda i: (i, 0)),
        out_shape=jax.ShapeDtypeStruct((m, q_w), BF16),
        compiler_params=pltpu.CompilerParams(
            dimension_semantics=("parallel",), vmem_limit_bytes=_vmem_limit(blk)),
        name="swa_attention",
    )(sinks, qkv, qkv, qkv, qkv, qkv, cos, sin, cos, sin)


def _softplus(z):
    return jnp.maximum(z, 0.0) + jnp.log1p(jnp.exp(-jnp.abs(z)))


def _lora_kernel(x_ref, prev_ref, mu_ref, w0_ref, a0_ref, w2_ref, a2_ref, g2_ref,
                 lw_ref, a_ref, g_ref, *, seq, d_decay, d_aaa):
    i = pl.program_id(0)
    tm = x_ref.shape[0]
    x = x_ref[...]
    last_prev = prev_ref[prev_ref.shape[0] - 1:, :]
    last_prev = jnp.where((i * tm) % seq == 0, 0.0, last_prev)
    row = lax.broadcasted_iota(jnp.int32, x.shape, 0)
    prev = jnp.where(row == 0, last_prev, pltpu.roll(x, 1, axis=0))
    xs = x + (prev - x) * mu_ref[...]
    p_w = jnp.tanh(xs[:, :d_decay]).astype(BF16)
    p_a = xs[:, d_decay:d_decay + d_aaa].astype(BF16)
    p_g = jax.nn.sigmoid(xs[:, d_decay + d_aaa:]).astype(BF16)
    w = -_softplus(-(w0_ref[...] + jnp.dot(p_w, w2_ref[...], preferred_element_type=F32))) - 0.5
    lw_ref[...] = -jnp.exp(w)
    a_ref[...] = jax.nn.sigmoid(
        a0_ref[...] + jnp.dot(p_a, a2_ref[...], preferred_element_type=F32))
    g_ref[...] = jnp.dot(p_g, g2_ref[...], preferred_element_type=F32)


def _lora(lora_raw, mu, w0, a0, w2, a2, g2, *, seq, d_decay, d_aaa, tm=256):
    m, wl = lora_raw.shape
    c = w2.shape[1]
    tm = min(tm, seq)
    sub = 8
    vec = lambda n: pl.BlockSpec((1, n), lambda i: (0, 0))
    full = lambda a: pl.BlockSpec(a.shape, lambda i: (0, 0))
    out = pl.BlockSpec((tm, c), lambda i: (i, 0))
    blk = (_nbytes((tm, wl), F32) + 3 * _nbytes((tm, c), F32)
           + _nbytes(w2.shape, BF16) + _nbytes(a2.shape, BF16) + _nbytes(g2.shape, BF16))
    return pl.pallas_call(
        functools.partial(_lora_kernel, seq=seq, d_decay=d_decay, d_aaa=d_aaa),
        grid=(m // tm,),
        in_specs=[
            pl.BlockSpec((tm, wl), lambda i: (i, 0)),
            pl.BlockSpec((sub, wl), lambda i: (jnp.maximum(i * (tm // sub) - 1, 0), 0)),
            vec(wl), vec(c), vec(c), full(w2), full(a2), full(g2),
        ],
        out_specs=[out, out, out],
        out_shape=[jax.ShapeDtypeStruct((m, c), F32)] * 3,
        compiler_params=pltpu.CompilerParams(
            dimension_semantics=("parallel",), vmem_limit_bytes=_vmem_limit(blk)),
        name="rwkv_lora",
    )(lora_raw, lora_raw, mu.reshape(1, wl), w0.reshape(1, c), a0.reshape(1, c), w2, a2, g2)


def _bf(x):
    return x.astype(BF16)


def _dot(a, b):
    return jnp.dot(_bf(a), _bf(b), preferred_element_type=F32)


def _dot_nt(a, b):
    return lax.dot_general(_bf(a), _bf(b), (((1,), (1,)), ((), ())), preferred_element_type=F32)


def _dot_tn(a, b):
    return lax.dot_general(_bf(a), _bf(b), (((0,), (0,)), ((), ())), preferred_element_type=F32)


def _split_dot(x, ones_bf, parts, ones_left=False):
    acc, rem = None, x
    for p in range(parts):
        hi = _bf(rem)
        ops = (ones_bf, hi) if ones_left else (hi, ones_bf)
        term = jnp.dot(*ops, preferred_element_type=F32)
        acc = term if acc is None else acc + term
        if p + 1 < parts:
            rem = rem - hi.astype(F32)
    return acc


def _wkv_kernel(r_ref, k_ref, v_ref, lw_ref, a_ref, g_ref,
                mur_ref, muk_ref, muv_ref, kk_ref, ka_ref, rk_ref, lnw_ref, lnb_ref,
                o_ref, rm_scr, yg_scr, bonus_scr, h_scr, *, n_chunks, unroll):
    c_len, two = CHUNK, 2 * CHUNK
    lane = lax.broadcasted_iota(jnp.int32, (c_len, PAIR), 1)
    m0 = (lane < RWKV_HEAD).astype(F32)
    m1 = 1.0 - m0
    row = lax.broadcasted_iota(jnp.int32, (two, two), 0)
    col = lax.broadcasted_iota(jnp.int32, (two, two), 1)
    same = (row >> 6) == (col >> 6)
    strict = same & ((row & 63) > (col & 63))
    incl = same & ((row & 63) >= (col & 63))
    eye = (row == col).astype(F32)
    head_ones = _bf(same.astype(F32))
    tri = lax.broadcasted_iota(jnp.int32, (c_len, c_len), 0) >= lax.broadcasted_iota(
        jnp.int32, (c_len, c_len), 1)
    tri_ones = _bf(tri.astype(F32))
    row0 = lax.broadcasted_iota(jnp.int32, (c_len, PAIR), 0) == 0

    mu_r, mu_k, mu_v = mur_ref[...], muk_ref[...], muv_ref[...]
    k_k, k_a, r_k = kk_ref[...], ka_ref[...], rk_ref[...]

    def stack(x):
        return jnp.concatenate([x * m0, x * m1], axis=0)

    def local(c, _):
        rows = pl.ds(pl.multiple_of(c * c_len, c_len), c_len)
        before = pl.ds(pl.multiple_of(jnp.maximum(c * c_len - SUBLANES, 0), SUBLANES), SUBLANES)
        keep_prev = (c > 0).astype(F32)

        def shift(ref, mu):
            x = ref[rows, :]
            last = ref[before, :][SUBLANES - 1:, :] * keep_prev
            prev = jnp.where(row0, last, pltpu.roll(x, 1, axis=0))
            return x + (prev - x) * mu

        rs, ks, vs = shift(r_ref, mu_r), shift(k_ref, mu_k), shift(v_ref, mu_v)
        lw, a = lw_ref[rows, :], a_ref[rows, :]

        kk = ks * k_k
        norm = jnp.sqrt(_split_dot(kk * kk, head_ones, 2))
        kk = kk / jnp.maximum(norm, 1e-12)
        kx = ks * (1.0 + (a - 1.0) * k_a)
        b = kk * a
        bonus_scr[rows, :] = _split_dot(rs * kx * r_k, head_ones, 2) * vs

        cum = _split_dot(lw, tri_ones, 3, ones_left=True)
        c_end = cum[c_len - 1:, :]
        e_pos, e_neg = jnp.exp(cum), jnp.exp(-cum)
        e_rem = jnp.exp(c_end - cum)
        at = -kk * jnp.exp(cum - lw)
        rt = rs * e_pos
        bt, kt = _bf(b * e_neg), _bf(kx * e_neg)
        at2, rt2, v2 = stack(at), stack(rt), _bf(stack(vs))
        bp2, kp2 = stack(b * e_rem), stack(kx * e_rem)

        s = _dot_nt(jnp.concatenate([at2, rt2], axis=0),
                    jnp.concatenate([bt, bt, kt, kt], axis=0))
        x = jnp.where(strict, s[:two, :two], 0.0)
        a_ak = jnp.where(strict, s[:two, two:], 0.0)
        rb = jnp.where(incl, s[two:, :two], 0.0)
        rkm = jnp.where(incl, s[two:, two:], 0.0)

        xk_pow = _dot(x, x)
        t_sum = eye + x
        steps = CHUNK.bit_length() - 3
        for _ in range(steps):
            both = _dot(jnp.concatenate([t_sum, xk_pow], axis=0), xk_pow)
            t_sum = t_sum + both[:two]
            xk_pow = both[two:]
        t_inv = t_sum + _dot(t_sum, xk_pow)

        z2 = _dot(a_ak, v2)
        pq = _dot(t_inv, jnp.concatenate([_bf(at2), _bf(z2)], axis=1))
        rhs = jnp.concatenate(
            [_bf(pq), jnp.concatenate([jnp.zeros_like(v2), v2], axis=1)], axis=0)
        n1 = _dot(jnp.concatenate([rb, rkm], axis=1), rhs)
        n2 = _dot_tn(jnp.concatenate([bp2, kp2], axis=0), rhs)
        rp2 = rt2 + n1[:, :two]
        m_blk = n2[:, :two] + eye * jnp.exp(c_end)
        base = pl.multiple_of(c * (2 * two), 2 * two)
        rm_scr[pl.ds(base, 2 * two), :] = _bf(jnp.concatenate([rp2, m_blk], axis=0))
        yg_scr[pl.ds(base, 2 * two), :] = jnp.concatenate([n1[:, two:], n2[:, two:]], axis=0)
        return 0

    lax.fori_loop(0, n_chunks, local, 0, unroll=unroll)

    ln_w, ln_b = lnw_ref[...], lnb_ref[...]
    inv_n = 1.0 / RWKV_HEAD
    h_scr[...] = jnp.zeros_like(h_scr)

    def serial(c, _):
        base = pl.multiple_of(c * (2 * two), 2 * two)
        out = jnp.dot(rm_scr[pl.ds(base, 2 * two), :], h_scr[...],
                      preferred_element_type=F32) + yg_scr[pl.ds(base, 2 * two), :]
        y = out[:c_len] + out[c_len:two]
        rows = pl.ds(pl.multiple_of(c * c_len, c_len), c_len)
        mean = _split_dot(y, head_ones, 2) * inv_n
        d = y - mean
        var = _split_dot(d * d, head_ones, 2) * inv_n
        yn = d * lax.rsqrt(var + GN_EPS) * ln_w + ln_b
        o_ref[rows, :] = ((yn + bonus_scr[rows, :]) * g_ref[rows, :]).astype(o_ref.dtype)
        h_scr[...] = _bf(out[two:])
        return 0

    lax.fori_loop(0, n_chunks, serial, 0)


def _wkv(rkv, lw, a, g, mu_rkv, k_k, k_a, r_k, ln_w, ln_b, *, batch, seq, unroll=2):
    m, c = lw.shape
    n_pairs = c // PAIR
    n_chunks = seq // CHUNK
    col = lambda off: pl.BlockSpec((seq, PAIR), lambda b, p: (b, p + off))
    vec = lambda off: pl.BlockSpec((1, PAIR), lambda b, p: (0, p + off))
    vecs = [x.reshape(1, c) for x in (k_k, k_a, r_k, ln_w, ln_b)]
    blk = 6 * _nbytes((seq, PAIR), F32) + _nbytes((seq, PAIR), BF16)
    scr = (_nbytes((seq * 4, PAIR), BF16) + _nbytes((seq * 4, PAIR), F32)
           + _nbytes((seq, PAIR), F32))
    return pl.pallas_call(
        functools.partial(_wkv_kernel, n_chunks=n_chunks, unroll=unroll),
        grid=(batch, n_pairs),
        in_specs=[col(0), col(n_pairs), col(2 * n_pairs), col(0), col(0), col(0),
                  vec(0), vec(n_pairs), vec(2 * n_pairs)] + [vec(0)] * 5,
        out_specs=col(0),
        out_shape=jax.ShapeDtypeStruct((m, c), BF16),
        scratch_shapes=[pltpu.VMEM((seq * 4, PAIR), BF16),
                        pltpu.VMEM((seq * 4, PAIR), F32),
                        pltpu.VMEM((seq, PAIR), F32),
                        pltpu.VMEM((PAIR, PAIR), BF16)],
        compiler_params=pltpu.CompilerParams(
            dimension_semantics=("parallel", "parallel"),
            vmem_limit_bytes=int(min(VMEM_CAP, 2 * blk + scr + 8 * 1024 * 1024))),
        name="wkv7_chunked",
    )(rkv, rkv, rkv, lw, a, g, mu_rkv, mu_rkv, mu_rkv, *vecs)


def _rope_tables(seq):
    pos = jnp.arange(seq, dtype=F32)
    inv_freq = ROPE_THETA ** (-jnp.arange(0, ATT_HEAD, 2, dtype=F32) / ATT_HEAD)
    ang = pos[:, None] * inv_freq[None, :]
    cos, sin = jnp.cos(ang), jnp.sin(ang)
    return jnp.concatenate([cos, cos], axis=-1), jnp.concatenate([-sin, sin], axis=-1)


def _pad_to(x, n, axis):
    pad = n - x.shape[axis]
    if pad == 0:
        return x
    widths = [(0, 0)] * x.ndim
    widths[axis] = (0, pad)
    return jnp.pad(x, widths)


def _round_up(n, mult):
    return -(-n // mult) * mult


def _layer(x2, batch, seq, norm_mix_pre, norm_mix_post, norm_ffn_pre, norm_ffn_post, w_in, b_qkv,
           att_sinks, mu_shift, w0, w2, a0, a2, g2, k_k, k_a, r_k, ln_x_w, ln_x_b,
           w_att_branch, w_rwkv_branch, w_out, w_ffn_gate, w_ffn_up, w_ffn_down):
    d = x2.shape[1]
    n_q = att_sinks.shape[0]
    q_w = n_q * ATT_HEAD
    qkv_w = b_qkv.shape[0]
    n_kv = (qkv_w - q_w) // (2 * ATT_HEAD)
    c = w0.shape[0]
    d_decay, d_aaa, d_gate = w2.shape[0], a2.shape[0], g2.shape[0]
    lora_w = d_decay + d_aaa + d_gate
    lora_pad = _round_up(lora_w, LANES)
    ffn = w_ffn_gate.shape[1]

    c0 = qkv_w
    w_att = _bf(w_in[:, :c0])
    w_rkv = _bf(w_in[:, c0:c0 + 3 * c])
    w_lora = _bf(_pad_to(w_in[:, c0 + 3 * c:c0 + 3 * c + lora_w], lora_pad, 1))
    w_gates = _bf(w_in[:, c0 + 3 * c + lora_w:])
    mu_rkv = mu_shift[:3 * c].reshape(1, 3 * c)
    mu_lora = _pad_to(mu_shift[3 * c:], lora_pad, 0)
    g2_pad = _bf(_pad_to(g2, lora_pad - d_decay - d_aaa, 0))

    (h,) = _norm_call(x2, norm_mix_pre, emit_main=False)

    qkv = _matmul(h, w_att, out_dtype=BF16, bias=b_qkv)
    rkv = _matmul(h, w_rkv, out_dtype=F32)
    lora_raw = _matmul(h, w_lora, out_dtype=F32, tn=lora_pad)
    gates = _matmul(h, w_gates, out_dtype=BF16, act="sigmoid")

    cos, sin = _rope_tables(seq)
    o_att = _attention(qkv, att_sinks.astype(F32), cos, sin, seq=seq, n_q=n_q, n_kv=n_kv)

    lw, a_gate, g_out = _lora(lora_raw, mu_lora, w0, a0, _bf(w2), _bf(a2), g2_pad,
                              seq=seq, d_decay=d_decay, d_aaa=d_aaa)
    o_rwkv = _wkv(rkv, lw, a_gate, g_out, mu_rkv, k_k, k_a, r_k.reshape(-1), ln_x_w, ln_x_b,
                  batch=batch, seq=seq)

    m_att = _matmul(o_att, _bf(w_att_branch), out_dtype=F32, tn=512, mul=gates, mul_col0=0)
    merged = _matmul(o_rwkv, _bf(w_rwkv_branch), out_dtype=BF16, tn=512, mul=gates, mul_col0=d,
                     add=m_att)
    mixed = _matmul(merged, _bf(w_out), out_dtype=F32)
    x1, h2 = _norm_call(mixed, norm_mix_post, res=x2, g_next=norm_ffn_pre)

    tn_ffn = 512
    ffn_pad = _round_up(ffn, 2 * tn_ffn)
    nt = ffn_pad // tn_ffn
    wg = _bf(_pad_to(w_ffn_gate, ffn_pad, 1)).reshape(d, nt, tn_ffn)
    wu = _bf(_pad_to(w_ffn_up, ffn_pad, 1)).reshape(d, nt, tn_ffn)
    w_gu = jnp.concatenate([wg, wu], axis=2).reshape(d, 2 * ffn_pad)
    w_dn = _bf(_pad_to(w_ffn_down, ffn_pad, 0))
    act = _matmul(h2, w_gu, out_dtype=BF16, act="swiglu", tn=tn_ffn)
    f = _matmul(act, w_dn, out_dtype=F32, tk=ffn_pad // 4)
    (out,) = _norm_call(f, norm_ffn_post, res=x1)
    return out


def kernel(x, norm_mix_pre, norm_mix_post, norm_ffn_pre, norm_ffn_post, w_in, b_qkv, att_sinks,
           mu_shift, w0, w2, a0, a2, g2, k_k, k_a, r_k, ln_x_w, ln_x_b,
           w_att_branch, w_rwkv_branch, w_out, w_ffn_gate, w_ffn_up, w_ffn_down):
    batch, seq, d = x.shape
    x2 = x.reshape(batch * seq, d)
    for l in range(w_in.shape[0]):
        x2 = _layer(x2, batch, seq, norm_mix_pre[l], norm_mix_post[l], norm_ffn_pre[l],
                    norm_ffn_post[l], w_in[l], b_qkv[l], att_sinks[l], mu_shift[l], w0[l], w2[l],
                    a0[l], a2[l], g2[l], k_k[l], k_a[l], r_k[l], ln_x_w[l], ln_x_b[l],
                    w_att_branch[l], w_rwkv_branch[l], w_out[l],
                    w_ffn_gate[l], w_ffn_up[l], w_ffn_down[l])
    return x2.reshape(batch, seq, d)
```

```python
import functools

import jax
import jax.numpy as jnp
from jax import lax
from jax.experimental import pallas as pl
from jax.experimental.pallas import tpu as pltpu

F32 = jnp.float32
BF16 = jnp.bfloat16

LANES = 128
SUBLANES = 8
V7X_VMEM_BYTES = 64 * 1024 * 1024
VMEM_CAP = V7X_VMEM_BYTES - 6 * 1024 * 1024

ATT_HEAD = 128
ATT_GROUP = 4
WINDOW = 128
ROPE_THETA = 10000.0
NEG_INF = -1e30
RWKV_HEAD = 64
CHUNK = 64
PAIR = 2 * RWKV_HEAD
GN_EPS = 64e-5
RMS_EPS = 1e-6


def _vmem_limit(block_bytes):
    return int(min(VMEM_CAP, 2 * block_bytes + 16 * 1024 * 1024))


def _nbytes(shape, dtype):
    n = 1
    for s in shape:
        n *= s
    return n * jnp.dtype(dtype).itemsize


def _rms(x, g):
    return x * lax.rsqrt(jnp.mean(x * x, axis=-1, keepdims=True) + RMS_EPS) * g


def _norm_kernel(*refs, has_res, emit_main, second_norm):
    it = iter(refs)
    x_ref, g_ref = next(it), next(it)
    res_ref = next(it) if has_res else None
    g2_ref = next(it) if second_norm else None
    main_ref = next(it) if emit_main else None
    next_ref = next(it) if (second_norm or not emit_main) else None
    y = _rms(x_ref[...], g_ref[...])
    if has_res:
        y = res_ref[...] + y
    if emit_main:
        main_ref[...] = y
    if second_norm:
        next_ref[...] = _rms(y, g2_ref[...]).astype(next_ref.dtype)
    elif not emit_main:
        next_ref[...] = y.astype(next_ref.dtype)


def _norm_call(x, g, res=None, g_next=None, *, emit_main=True, tm=256):
    m, d = x.shape
    tm = min(tm, m)
    second_norm = g_next is not None
    assert emit_main or not second_norm
    row = pl.BlockSpec((tm, d), lambda i: (i, 0))
    vec = pl.BlockSpec((1, d), lambda i: (0, 0))
    args, specs = [x, g.reshape(1, d)], [row, vec]
    if res is not None:
        args.append(res); specs.append(row)
    if second_norm:
        args.append(g_next.reshape(1, d)); specs.append(vec)
    out_shape, out_specs = [], []
    if emit_main:
        out_shape.append(jax.ShapeDtypeStruct((m, d), F32)); out_specs.append(row)
    if second_norm or not emit_main:
        out_shape.append(jax.ShapeDtypeStruct((m, d), BF16)); out_specs.append(row)
    blk = _nbytes((tm, d), F32) * (len(args) + len(out_shape))
    outs = pl.pallas_call(
        functools.partial(_norm_kernel, has_res=res is not None, emit_main=emit_main,
                          second_norm=second_norm),
        grid=(m // tm,),
        in_specs=specs, out_specs=out_specs, out_shape=out_shape,
        compiler_params=pltpu.CompilerParams(
            dimension_semantics=("parallel",), vmem_limit_bytes=_vmem_limit(blk)),
        name="rmsnorm",
    )(*args)
    return outs


def _mm_kernel(*refs, nk, has_bias, act, has_mul, has_add):
    it = iter(refs)
    a_ref, w_ref = next(it), next(it)
    bias_ref = next(it) if has_bias else None
    mul_ref = next(it) if has_mul else None
    add_ref = next(it) if has_add else None
    o_ref = next(it)
    acc_ref = next(it) if nk > 1 else None

    part = jnp.dot(a_ref[...], w_ref[...], preferred_element_type=F32)

    def finish(y):
        if has_bias:
            y = y + bias_ref[...]
        if act == "sigmoid":
            y = jax.nn.sigmoid(y)
        elif act == "swiglu":
            half = y.shape[1] // 2
            gate, up = y[:, :half], y[:, half:]
            y = gate * jax.nn.sigmoid(gate) * up
        if has_mul:
            y = y * mul_ref[...].astype(F32)
        if has_add:
            y = y + add_ref[...].astype(F32)
        o_ref[...] = y.astype(o_ref.dtype)

    if nk == 1:
        finish(part)
    else:
        k = pl.program_id(2)

        @pl.when(k == 0)
        def _():
            acc_ref[...] = part

        @pl.when(jnp.logical_and(k > 0, k < nk - 1))
        def _():
            acc_ref[...] += part

        @pl.when(k == nk - 1)
        def _():
            finish(acc_ref[...] + part)


def _matmul(a, w, *, out_dtype, tm=1024, tn=1024, tk=None, bias=None, act=None,
            mul=None, mul_col0=0, add=None):
    m, kdim = a.shape
    n_w = w.shape[1]
    tm = min(tm, m)
    tk = kdim if tk is None else tk
    nk = kdim // tk
    assert m % tm == 0 and kdim % tk == 0
    tn_w = tn * 2 if act == "swiglu" else tn
    assert n_w % tn_w == 0 and tn % LANES == 0
    n_out = n_w // 2 if act == "swiglu" else n_w
    grid = (m // tm, n_w // tn_w, nk)

    args = [a, w]
    specs = [pl.BlockSpec((tm, tk), lambda i, j, k: (i, k)),
             pl.BlockSpec((tk, tn_w), lambda i, j, k: (k, j))]
    blk = _nbytes((tm, tk), a.dtype) + _nbytes((tk, tn_w), w.dtype)
    if bias is not None:
        args.append(bias.reshape(1, n_w).astype(F32))
        specs.append(pl.BlockSpec((1, tn_w), lambda i, j, k: (0, j)))
    if mul is not None:
        assert mul_col0 % tn == 0
        off = mul_col0 // tn
        args.append(mul)
        specs.append(pl.BlockSpec((tm, tn), lambda i, j, k: (i, j + off)))
        blk += _nbytes((tm, tn), mul.dtype)
    if add is not None:
        args.append(add)
        specs.append(pl.BlockSpec((tm, tn), lambda i, j, k: (i, j)))
        blk += _nbytes((tm, tn), add.dtype)
    blk += _nbytes((tm, tn), out_dtype)
    scratch = [pltpu.VMEM((tm, tn_w), F32)] if nk > 1 else []
    extra = _nbytes((tm, tn_w), F32) * (4 if nk > 1 else 3)

    return pl.pallas_call(
        functools.partial(_mm_kernel, nk=nk, has_bias=bias is not None, act=act,
                          has_mul=mul is not None, has_add=add is not None),
        grid=grid,
        in_specs=specs,
        out_specs=pl.BlockSpec((tm, tn), lambda i, j, k: (i, j)),
        out_shape=jax.ShapeDtypeStruct((m, n_out), out_dtype),
        scratch_shapes=scratch,
        compiler_params=pltpu.CompilerParams(
            dimension_semantics=("parallel", "parallel", "arbitrary"),
            vmem_limit_bytes=int(min(VMEM_CAP, 2 * blk + extra + 4 * 1024 * 1024))),
        name="matmul_" + (act or "plain"),
    )(*args)


def _attn_kernel(sinks_ref, q_ref, kc_ref, kp_ref, vc_ref, vp_ref,
                 cosc_ref, sinc_ref, cosp_ref, sinp_ref, o_ref, *, nb, n_kv):
    i = pl.program_id(0)
    first = (i % nb) == 0
    blk = WINDOW
    scale = ATT_HEAD ** -0.5

    def rope(t, cos, sin):
        t = t.astype(F32)
        return t * cos + pltpu.roll(t, ATT_HEAD // 2, axis=1) * sin

    cos_c, sin_c = cosc_ref[...], sinc_ref[...]
    cos_p, sin_p = cosp_ref[...], sinp_ref[...]

    rows = ATT_GROUP * blk
    qi = lax.broadcasted_iota(jnp.int32, (rows, 2 * blk), 0) & (blk - 1)
    kj = lax.broadcasted_iota(jnp.int32, (rows, 2 * blk), 1)
    lo = jnp.where(first, blk, 0)
    valid = (kj > qi) & (kj <= qi + blk) & (kj >= lo)
    grp = lax.broadcasted_iota(jnp.int32, (rows, 1), 0) >> (blk.bit_length() - 1)

    for h in range(n_kv):
        cs = slice(h * ATT_HEAD, (h + 1) * ATT_HEAD)
        k_band = jnp.concatenate(
            [rope(kp_ref[:, cs], cos_p, sin_p), rope(kc_ref[:, cs], cos_c, sin_c)],
            axis=0).astype(BF16)
        v_band = jnp.concatenate([vp_ref[:, cs], vc_ref[:, cs]], axis=0)
        q_parts, sink = [], jnp.zeros((rows, 1), F32)
        for g in range(ATT_GROUP):
            hq = h * ATT_GROUP + g
            qs = slice(hq * ATT_HEAD, (hq + 1) * ATT_HEAD)
            q_parts.append((rope(q_ref[:, qs], cos_c, sin_c) * scale).astype(BF16))
            sink = jnp.where(grp == g, sinks_ref[hq], sink)
        q_all = jnp.concatenate(q_parts, axis=0)
        s = lax.dot_general(q_all, k_band, (((1,), (1,)), ((), ())),
                            preferred_element_type=F32)
        s = jnp.where(valid, s, NEG_INF)
        m = jnp.maximum(jnp.max(s, axis=-1, keepdims=True), sink)
        p = jnp.exp(s - m)
        denom = jnp.sum(p, axis=-1, keepdims=True) + jnp.exp(sink - m)
        o = jnp.dot(p.astype(BF16), v_band, preferred_element_type=F32) / denom
        for g in range(ATT_GROUP):
            hq = h * ATT_GROUP + g
            o_ref[:, hq * ATT_HEAD:(hq + 1) * ATT_HEAD] = (
                o[g * blk:(g + 1) * blk].astype(o_ref.dtype))


def _attention(qkv, sinks, cos, sin, *, seq, n_q, n_kv):
    m = qkv.shape[0]
    nb = seq // WINDOW
    q_w, kv_w = n_q * ATT_HEAD, n_kv * ATT_HEAD
    k_blk0, v_blk0 = q_w // kv_w, q_w // kv_w + 1
    prev = lambda i: jnp.maximum(i - 1, 0)
    tab_c = pl.BlockSpec((WINDOW, ATT_HEAD), lambda i: (i % nb, 0))
    tab_p = pl.BlockSpec((WINDOW, ATT_HEAD), lambda i: (jnp.maximum(i % nb - 1, 0), 0))
    blk = (_nbytes((WINDOW, q_w), BF16) * 2 + 4 * _nbytes((WINDOW, kv_w), BF16)
           + 4 * _nbytes((WINDOW, ATT_HEAD), F32))
    return pl.pallas_call(
        functools.partial(_attn_kernel, nb=nb, n_kv=n_kv),
        grid=(m // WINDOW,),
        in_specs=[
            pl.BlockSpec(memory_space=pltpu.SMEM),
            pl.BlockSpec((WINDOW, q_w), lambda i: (i, 0)),
            pl.BlockSpec((WINDOW, kv_w), lambda i: (i, k_blk0)),
            pl.BlockSpec((WINDOW, kv_w), lambda i: (prev(i), k_blk0)),
            pl.BlockSpec((WINDOW, kv_w), lambda i: (i, v_blk0)),
            pl.BlockSpec((WINDOW, kv_w), lambda i: (prev(i), v_blk0)),
            tab_c, tab_c, tab_p, tab_p,
        ],
        out_specs=pl.BlockSpec((WINDOW, q_w), lambda i: (i, 0)),
        out_shape=jax.ShapeDtypeStruct((m, q_w), BF16),
        compiler_params=pltpu.CompilerParams(
            dimension_semantics=("parallel",), vmem_limit_bytes=_vmem_limit(blk)),
        name="swa_attention",
    )(sinks, qkv, qkv, qkv, qkv, qkv, cos, sin, cos, sin)


def _softplus(z):
    return jnp.maximum(z, 0.0) + jnp.log1p(jnp.exp(-jnp.abs(z)))


def _lora_kernel(x_ref, prev_ref, mu_ref, w0_ref, a0_ref, w2_ref, a2_ref, g2_ref,
                 lw_ref, a_ref, g_ref, *, seq, d_decay, d_aaa):
    i = pl.program_id(0)
    tm = x_ref.shape[0]
    x = x_ref[...]
    last_prev = prev_ref[prev_ref.shape[0] - 1:, :]
    last_prev = jnp.where((i * tm) % seq == 0, 0.0, last_prev)
    row = lax.broadcasted_iota(jnp.int32, x.shape, 0)
    prev = jnp.where(row == 0, last_prev, pltpu.roll(x, 1, axis=0))
    xs = x + (prev - x) * mu_ref[...]
    p_w = jnp.tanh(xs[:, :d_decay]).astype(BF16)
    p_a = xs[:, d_decay:d_decay + d_aaa].astype(BF16)
    p_g = jax.nn.sigmoid(xs[:, d_decay + d_aaa:]).astype(BF16)
    w = -_softplus(-(w0_ref[...] + jnp.dot(p_w, w2_ref[...], preferred_element_type=F32))) - 0.5
    lw_ref[...] = -jnp.exp(w)
    a_ref[...] = jax.nn.sigmoid(
        a0_ref[...] + jnp.dot(p_a, a2_ref[...], preferred_element_type=F32))
    g_ref[...] = jnp.dot(p_g, g2_ref[...], preferred_element_type=F32)


def _lora(lora_raw, mu, w0, a0, w2, a2, g2, *, seq, d_decay, d_aaa, tm=256):
    m, wl = lora_raw.shape
    c = w2.shape[1]
    tm = min(tm, seq)
    sub = 8
    vec = lambda n: pl.BlockSpec((1, n), lambda i: (0, 0))
    full = lambda a: pl.BlockSpec(a.shape, lambda i: (0, 0))
    out = pl.BlockSpec((tm, c), lambda i: (i, 0))
    blk = (_nbytes((tm, wl), F32) + 3 * _nbytes((tm, c), F32)
           + _nbytes(w2.shape, BF16) + _nbytes(a2.shape, BF16) + _nbytes(g2.shape, BF16))
    return pl.pallas_call(
        functools.partial(_lora_kernel, seq=seq, d_decay=d_decay, d_aaa=d_aaa),
        grid=(m // tm,),
        in_specs=[
            pl.BlockSpec((tm, wl), lambda i: (i, 0)),
            pl.BlockSpec((sub, wl), lambda i: (jnp.maximum(i * (tm // sub) - 1, 0), 0)),
            vec(wl), vec(c), vec(c), full(w2), full(a2), full(g2),
        ],
        out_specs=[out, out, out],
        out_shape=[jax.ShapeDtypeStruct((m, c), F32)] * 3,
        compiler_params=pltpu.CompilerParams(
            dimension_semantics=("parallel",), vmem_limit_bytes=_vmem_limit(blk)),
        name="rwkv_lora",
    )(lora_raw, lora_raw, mu.reshape(1, wl), w0.reshape(1, c), a0.reshape(1, c), w2, a2, g2)


def _bf(x):
    return x.astype(BF16)


def _dot(a, b):
    return jnp.dot(_bf(a), _bf(b), preferred_element_type=F32)


def _dot_nt(a, b):
    return lax.dot_general(_bf(a), _bf(b), (((1,), (1,)), ((), ())), preferred_element_type=F32)


def _dot_tn(a, b):
    return lax.dot_general(_bf(a), _bf(b), (((0,), (0,)), ((), ())), preferred_element_type=F32)


def _split_dot(x, ones_bf, parts, ones_left=False):
    acc, rem = None, x
    for p in range(parts):
        hi = _bf(rem)
        ops = (ones_bf, hi) if ones_left else (hi, ones_bf)
        term = jnp.dot(*ops, preferred_element_type=F32)
        acc = term if acc is None else acc + term
        if p + 1 < parts:
            rem = rem - hi.astype(F32)
    return acc


def _wkv_kernel(r_ref, k_ref, v_ref, lw_ref, a_ref, g_ref,
                mur_ref, muk_ref, muv_ref, kk_ref, ka_ref, rk_ref, lnw_ref, lnb_ref,
                o_ref, rm_scr, yg_scr, bonus_scr, y_scr, h_scr, *, n_chunks, group):
    c_len, two = CHUNK, 2 * CHUNK
    g_rows = group * c_len
    shift_c = CHUNK.bit_length() - 1
    lane = lax.broadcasted_iota(jnp.int32, (c_len, PAIR), 1)
    m0 = (lane < RWKV_HEAD).astype(F32)
    m1 = 1.0 - m0
    row = lax.broadcasted_iota(jnp.int32, (two, two), 0)
    col = lax.broadcasted_iota(jnp.int32, (two, two), 1)
    same = (row >> shift_c) == (col >> shift_c)
    strict = same & ((row & (CHUNK - 1)) > (col & (CHUNK - 1)))
    incl = same & ((row & (CHUNK - 1)) >= (col & (CHUNK - 1)))
    eye = (row == col).astype(F32)
    head_ones = _bf(same.astype(F32))
    grow = lax.broadcasted_iota(jnp.int32, (g_rows, g_rows), 0)
    gcol = lax.broadcasted_iota(jnp.int32, (g_rows, g_rows), 1)
    tri_ones = _bf((((grow >> shift_c) == (gcol >> shift_c)) & (grow >= gcol)).astype(F32))
    row0 = lax.broadcasted_iota(jnp.int32, (g_rows, PAIR), 0) == 0

    mu_r, mu_k, mu_v = mur_ref[...], muk_ref[...], muv_ref[...]
    k_k, k_a, r_k = kk_ref[...], ka_ref[...], rk_ref[...]

    def stack(x):
        return jnp.concatenate([x * m0, x * m1], axis=0)

    def local(i, _):
        rows = pl.ds(pl.multiple_of(i * g_rows, g_rows), g_rows)
        before = pl.ds(pl.multiple_of(jnp.maximum(i * g_rows - SUBLANES, 0), SUBLANES), SUBLANES)
        keep_prev = (i > 0).astype(F32)

        def shift(ref, mu):
            x = ref[rows, :]
            last = ref[before, :][SUBLANES - 1:, :] * keep_prev
            prev = jnp.where(row0, last, pltpu.roll(x, 1, axis=0))
            return x + (prev - x) * mu

        rs, ks, vs = shift(r_ref, mu_r), shift(k_ref, mu_k), shift(v_ref, mu_v)
        lw, a = lw_ref[rows, :], a_ref[rows, :]

        kk = ks * k_k
        norm = jnp.sqrt(_split_dot(kk * kk, head_ones, 2))
        kk = kk / jnp.maximum(norm, 1e-12)
        kx = ks * (1.0 + (a - 1.0) * k_a)
        b = kk * a
        bonus_scr[rows, :] = _split_dot(rs * kx * r_k, head_ones, 2) * vs

        cum = _split_dot(lw, tri_ones, 3, ones_left=True)
        e_neg = jnp.exp(-cum)
        at_g = -kk * jnp.exp(cum - lw)
        rt_g = rs * jnp.exp(cum)
        bt_g, kt_g = _bf(b * e_neg), _bf(kx * e_neg)

        st = []
        for j in range(group):
            sl = slice(j * c_len, (j + 1) * c_len)
            c_end = cum[(j + 1) * c_len - 1:(j + 1) * c_len, :]
            e_rem = jnp.exp(c_end - cum[sl])
            at2, rt2 = stack(at_g[sl]), stack(rt_g[sl])
            st.append(dict(
                at2=at2, rt2=rt2, v2=_bf(stack(vs[sl])), d_end=jnp.exp(c_end),
                bk2=jnp.concatenate([stack(b[sl] * e_rem), stack(kx[sl] * e_rem)], axis=0),
                s=_dot_nt(jnp.concatenate([at2, rt2], axis=0),
                          jnp.concatenate([bt_g[sl], bt_g[sl], kt_g[sl], kt_g[sl]], axis=0))))
        for d in st:
            s = d.pop("s")
            x = jnp.where(strict, s[:two, :two], 0.0)
            a_ak = jnp.where(strict, s[:two, two:], 0.0)
            d["rbk"] = jnp.concatenate([jnp.where(incl, s[two:, :two], 0.0),
                                        jnp.where(incl, s[two:, two:], 0.0)], axis=1)
            d["t_sum"] = eye + x
            d["x_pow"] = _dot(x, x)
            d["z2"] = _dot(a_ak, d["v2"])
        for _ in range(CHUNK.bit_length() - 3):
            for d in st:
                both = _dot(jnp.concatenate([d["t_sum"], d["x_pow"]], axis=0), d["x_pow"])
                d["t_sum"] = d["t_sum"] + both[:two]
                d["x_pow"] = both[two:]
        for d in st:
            d["t_inv"] = d["t_sum"] + _dot(d["t_sum"], d["x_pow"])
        for d in st:
            pq = _dot(d["t_inv"], jnp.concatenate([_bf(d["at2"]), _bf(d["z2"])], axis=1))
            d["rhs"] = jnp.concatenate(
                [_bf(pq), jnp.concatenate([jnp.zeros_like(d["v2"]), d["v2"]], axis=1)], axis=0)
        for d in st:
            d["n1"] = _dot(d["rbk"], d["rhs"])
            d["n2"] = _dot_tn(d["bk2"], d["rhs"])
        for j, d in enumerate(st):
            n1, n2 = d["n1"], d["n2"]
            rp2 = d["rt2"] + n1[:, :two]
            m_blk = n2[:, :two] + eye * d["d_end"]
            base = pl.multiple_of((i * group + j) * (2 * two), 2 * two)
            rm_scr[pl.ds(base, 2 * two), :] = _bf(jnp.concatenate([rp2, m_blk], axis=0))
            yg_scr[pl.ds(base, 2 * two), :] = jnp.concatenate([n1[:, two:], n2[:, two:]], axis=0)
        return 0

    lax.fori_loop(0, n_chunks // group, local, 0)

    h_scr[...] = jnp.zeros_like(h_scr)

    def serial(c, _):
        base = pl.multiple_of(c * (2 * two), 2 * two)
        out = jnp.dot(rm_scr[pl.ds(base, 2 * two), :], h_scr[...],
                      preferred_element_type=F32) + yg_scr[pl.ds(base, 2 * two), :]
        rows = pl.ds(pl.multiple_of(c * c_len, c_len), c_len)
        y_scr[rows, :] = out[:c_len] + out[c_len:two]
        h_scr[...] = _bf(out[two:])
        return 0

    lax.fori_loop(0, n_chunks, serial, 0)

    ln_w, ln_b = lnw_ref[...], lnb_ref[...]
    inv_n = 1.0 / RWKV_HEAD

    def finish(i, _):
        rows = pl.ds(pl.multiple_of(i * g_rows, g_rows), g_rows)
        y = y_scr[rows, :]
        mean = _split_dot(y, head_ones, 2) * inv_n
        d = y - mean
        var = _split_dot(d * d, head_ones, 2) * inv_n
        yn = d * lax.rsqrt(var + GN_EPS) * ln_w + ln_b
        o_ref[rows, :] = ((yn + bonus_scr[rows, :]) * g_ref[rows, :]).astype(o_ref.dtype)
        return 0

    lax.fori_loop(0, n_chunks // group, finish, 0)


def _wkv(rkv, lw, a, g, mu_rkv, k_k, k_a, r_k, ln_w, ln_b, *, batch, seq, group=4):
    m, c = lw.shape
    n_pairs = c // PAIR
    n_chunks = seq // CHUNK
    assert seq % (group * CHUNK) == 0
    col = lambda off: pl.BlockSpec((seq, PAIR), lambda b, p: (b, p + off))
    vec = lambda off: pl.BlockSpec((1, PAIR), lambda b, p: (0, p + off))
    vecs = [x.reshape(1, c) for x in (k_k, k_a, r_k, ln_w, ln_b)]
    blk = 6 * _nbytes((seq, PAIR), F32) + _nbytes((seq, PAIR), BF16)
    scr = (_nbytes((seq * 4, PAIR), BF16) + _nbytes((seq * 4, PAIR), F32)
           + 2 * _nbytes((seq, PAIR), F32))
    return pl.pallas_call(
        functools.partial(_wkv_kernel, n_chunks=n_chunks, group=group),
        grid=(batch, n_pairs),
        in_specs=[col(0), col(n_pairs), col(2 * n_pairs), col(0), col(0), col(0),
                  vec(0), vec(n_pairs), vec(2 * n_pairs)] + [vec(0)] * 5,
        out_specs=col(0),
        out_shape=jax.ShapeDtypeStruct((m, c), BF16),
        scratch_shapes=[pltpu.VMEM((seq * 4, PAIR), BF16),
                        pltpu.VMEM((seq * 4, PAIR), F32),
                        pltpu.VMEM((seq, PAIR), F32),
                        pltpu.VMEM((seq, PAIR), F32),
                        pltpu.VMEM((PAIR, PAIR), BF16)],
        compiler_params=pltpu.CompilerParams(
            dimension_semantics=("parallel", "parallel"),
            vmem_limit_bytes=int(min(VMEM_CAP, 2 * blk + scr + 8 * 1024 * 1024))),
        name="wkv7_chunked",
    )(rkv, rkv, rkv, lw, a, g, mu_rkv, mu_rkv, mu_rkv, *vecs)


def _rope_tables(seq):
    pos = jnp.arange(seq, dtype=F32)
    inv_freq = ROPE_THETA ** (-jnp.arange(0, ATT_HEAD, 2, dtype=F32) / ATT_HEAD)
    ang = pos[:, None] * inv_freq[None, :]
    cos, sin = jnp.cos(ang), jnp.sin(ang)
    return jnp.concatenate([cos, cos], axis=-1), jnp.concatenate([-sin, sin], axis=-1)


def _pad_to(x, n, axis):
    pad = n - x.shape[axis]
    if pad == 0:
        return x
    widths = [(0, 0)] * x.ndim
    widths[axis] = (0, pad)
    return jnp.pad(x, widths)


def _round_up(n, mult):
    return -(-n // mult) * mult


def _layer(x2, batch, seq, norm_mix_pre, norm_mix_post, norm_ffn_pre, norm_ffn_post, w_in, b_qkv,
           att_sinks, mu_shift, w0, w2, a0, a2, g2, k_k, k_a, r_k, ln_x_w, ln_x_b,
           w_att_branch, w_rwkv_branch, w_out, w_ffn_gate, w_ffn_up, w_ffn_down):
    d = x2.shape[1]
    n_q = att_sinks.shape[0]
    q_w = n_q * ATT_HEAD
    qkv_w = b_qkv.shape[0]
    n_kv = (qkv_w - q_w) // (2 * ATT_HEAD)
    c = w0.shape[0]
    d_decay, d_aaa, d_gate = w2.shape[0], a2.shape[0], g2.shape[0]
    lora_w = d_decay + d_aaa + d_gate
    lora_pad = _round_up(lora_w, LANES)
    ffn = w_ffn_gate.shape[1]

    c0 = qkv_w
    w_att = _bf(w_in[:, :c0])
    w_rkv = _bf(w_in[:, c0:c0 + 3 * c])
    w_lora = _bf(_pad_to(w_in[:, c0 + 3 * c:c0 + 3 * c + lora_w], lora_pad, 1))
    w_gates = _bf(w_in[:, c0 + 3 * c + lora_w:])
    mu_rkv = mu_shift[:3 * c].reshape(1, 3 * c)
    mu_lora = _pad_to(mu_shift[3 * c:], lora_pad, 0)
    g2_pad = _bf(_pad_to(g2, lora_pad - d_decay - d_aaa, 0))

    (h,) = _norm_call(x2, norm_mix_pre, emit_main=False)

    qkv = _matmul(h, w_att, out_dtype=BF16, bias=b_qkv)
    rkv = _matmul(h, w_rkv, out_dtype=F32)
    lora_raw = _matmul(h, w_lora, out_dtype=F32, tn=lora_pad)
    gates = _matmul(h, w_gates, out_dtype=BF16, act="sigmoid")

    cos, sin = _rope_tables(seq)
    o_att = _attention(qkv, att_sinks.astype(F32), cos, sin, seq=seq, n_q=n_q, n_kv=n_kv)

    lw, a_gate, g_out = _lora(lora_raw, mu_lora, w0, a0, _bf(w2), _bf(a2), g2_pad,
                              seq=seq, d_decay=d_decay, d_aaa=d_aaa)
    o_rwkv = _wkv(rkv, lw, a_gate, g_out, mu_rkv, k_k, k_a, r_k.reshape(-1), ln_x_w, ln_x_b,
                  batch=batch, seq=seq)

    m_att = _matmul(o_att, _bf(w_att_branch), out_dtype=F32, tn=512, mul=gates, mul_col0=0)
    merged = _matmul(o_rwkv, _bf(w_rwkv_branch), out_dtype=BF16, tn=512, mul=gates, mul_col0=d,
                     add=m_att)
    mixed = _matmul(merged, _bf(w_out), out_dtype=F32)
    x1, h2 = _norm_call(mixed, norm_mix_post, res=x2, g_next=norm_ffn_pre)

    tn_ffn = 512
    ffn_pad = _round_up(ffn, 2 * tn_ffn)
    nt = ffn_pad // tn_ffn
    wg = _bf(_pad_to(w_ffn_gate, ffn_pad, 1)).reshape(d, nt, tn_ffn)
    wu = _bf(_pad_to(w_ffn_up, ffn_pad, 1)).reshape(d, nt, tn_ffn)
    w_gu = jnp.concatenate([wg, wu], axis=2).reshape(d, 2 * ffn_pad)
    w_dn = _bf(_pad_to(w_ffn_down, ffn_pad, 0))
    act = _matmul(h2, w_gu, out_dtype=BF16, act="swiglu", tn=tn_ffn)
    f = _matmul(act, w_dn, out_dtype=F32, tk=ffn_pad // 4)
    (out,) = _norm_call(f, norm_ffn_post, res=x1)
    return out


def kernel(x, norm_mix_pre, norm_mix_post, norm_ffn_pre, norm_ffn_post, w_in, b_qkv, att_sinks,
           mu_shift, w0, w2, a0, a2, g2, k_k, k_a, r_k, ln_x_w, ln_x_b,
           w_att_branch, w_rwkv_branch, w_out, w_ffn_gate, w_ffn_up, w_ffn_down):
    batch, seq, d = x.shape
    x2 = x.reshape(batch * seq, d)
    for l in range(w_in.shape[0]):
        x2 = _layer(x2, batch, seq, norm_mix_pre[l], norm_mix_post[l], norm_ffn_pre[l],
                    norm_ffn_post[l], w_in[l], b_qkv[l], att_sinks[l], mu_shift[l], w0[l], w2[l],
                    a0[l], a2[l], g2[l], k_k[l], k_a[l], r_k[l], ln_x_w[l], ln_x_b[l],
                    w_att_branch[l], w_rwkv_branch[l], w_out[l],
                    w_ffn_gate[l], w_ffn_up[l], w_ffn_down[l])
    return x2.reshape(batch, seq, d)
```

```python
import functools

import jax
import jax.numpy as jnp
from jax import lax
from jax.experimental import pallas as pl
from jax.experimental.pallas import tpu as pltpu

F32 = jnp.float32
BF16 = jnp.bfloat16

LANES = 128
SUBLANES = 8
V7X_VMEM_BYTES = 64 * 1024 * 1024
VMEM_CAP = V7X_VMEM_BYTES - 6 * 1024 * 1024

ATT_HEAD = 128
ATT_GROUP = 4
WINDOW = 128
ROPE_THETA = 10000.0
NEG_INF = -1e30
RWKV_HEAD = 64
CHUNK = 64
PAIR = 2 * RWKV_HEAD
GN_EPS = 64e-5
RMS_EPS = 1e-6


def _vmem_limit(block_bytes):
    return int(min(VMEM_CAP, 2 * block_bytes + 16 * 1024 * 1024))


def _nbytes(shape, dtype):
    n = 1
    for s in shape:
        n *= s
    return n * jnp.dtype(dtype).itemsize


def _rms(x, g):
    return x * lax.rsqrt(jnp.mean(x * x, axis=-1, keepdims=True) + RMS_EPS) * g


def _norm_kernel(*refs, has_res, emit_main, second_norm):
    it = iter(refs)
    x_ref, g_ref = next(it), next(it)
    res_ref = next(it) if has_res else None
    g2_ref = next(it) if second_norm else None
    main_ref = next(it) if emit_main else None
    next_ref = next(it) if (second_norm or not emit_main) else None
    y = _rms(x_ref[...], g_ref[...])
    if has_res:
        y = res_ref[...] + y
    if emit_main:
        main_ref[...] = y
    if second_norm:
        next_ref[...] = _rms(y, g2_ref[...]).astype(next_ref.dtype)
    elif not emit_main:
        next_ref[...] = y.astype(next_ref.dtype)


def _norm_call(x, g, res=None, g_next=None, *, emit_main=True, tm=256):
    m, d = x.shape
    tm = min(tm, m)
    second_norm = g_next is not None
    assert emit_main or not second_norm
    row = pl.BlockSpec((tm, d), lambda i: (i, 0))
    vec = pl.BlockSpec((1, d), lambda i: (0, 0))
    args, specs = [x, g.reshape(1, d)], [row, vec]
    if res is not None:
        args.append(res); specs.append(row)
    if second_norm:
        args.append(g_next.reshape(1, d)); specs.append(vec)
    out_shape, out_specs = [], []
    if emit_main:
        out_shape.append(jax.ShapeDtypeStruct((m, d), F32)); out_specs.append(row)
    if second_norm or not emit_main:
        out_shape.append(jax.ShapeDtypeStruct((m, d), BF16)); out_specs.append(row)
    blk = _nbytes((tm, d), F32) * (len(args) + len(out_shape))
    outs = pl.pallas_call(
        functools.partial(_norm_kernel, has_res=res is not None, emit_main=emit_main,
                          second_norm=second_norm),
        grid=(m // tm,),
        in_specs=specs, out_specs=out_specs, out_shape=out_shape,
        compiler_params=pltpu.CompilerParams(
            dimension_semantics=("parallel",), vmem_limit_bytes=_vmem_limit(blk)),
        name="rmsnorm",
    )(*args)
    return outs


def _mm_kernel(*refs, nk, has_bias, act, has_mul, has_add):
    it = iter(refs)
    a_ref, w_ref = next(it), next(it)
    bias_ref = next(it) if has_bias else None
    mul_ref = next(it) if has_mul else None
    add_ref = next(it) if has_add else None
    o_ref = next(it)
    acc_ref = next(it) if nk > 1 else None

    part = jnp.dot(a_ref[...], w_ref[...], preferred_element_type=F32)

    def finish(y):
        if has_bias:
            y = y + bias_ref[...]
        if act == "sigmoid":
            y = jax.nn.sigmoid(y)
        elif act == "swiglu":
            half = y.shape[1] // 2
            gate, up = y[:, :half], y[:, half:]
            y = gate * jax.nn.sigmoid(gate) * up
        if has_mul:
            y = y * mul_ref[...].astype(F32)
        if has_add:
            y = y + add_ref[...].astype(F32)
        o_ref[...] = y.astype(o_ref.dtype)

    if nk == 1:
        finish(part)
    else:
        k = pl.program_id(2)

        @pl.when(k == 0)
        def _():
            acc_ref[...] = part

        @pl.when(jnp.logical_and(k > 0, k < nk - 1))
        def _():
            acc_ref[...] += part

        @pl.when(k == nk - 1)
        def _():
            finish(acc_ref[...] + part)


def _matmul(a, w, *, out_dtype, tm=1024, tn=1024, tk=None, bias=None, act=None,
            mul=None, mul_col0=0, add=None):
    m, kdim = a.shape
    n_w = w.shape[1]
    tm = min(tm, m)
    tk = kdim if tk is None else tk
    nk = kdim // tk
    assert m % tm == 0 and kdim % tk == 0
    tn_w = tn * 2 if act == "swiglu" else tn
    assert n_w % tn_w == 0 and tn % LANES == 0
    n_out = n_w // 2 if act == "swiglu" else n_w
    grid = (m // tm, n_w // tn_w, nk)

    args = [a, w]
    specs = [pl.BlockSpec((tm, tk), lambda i, j, k: (i, k)),
             pl.BlockSpec((tk, tn_w), lambda i, j, k: (k, j))]
    blk = _nbytes((tm, tk), a.dtype) + _nbytes((tk, tn_w), w.dtype)
    if bias is not None:
        args.append(bias.reshape(1, n_w).astype(F32))
        specs.append(pl.BlockSpec((1, tn_w), lambda i, j, k: (0, j)))
    if mul is not None:
        assert mul_col0 % tn == 0
        off = mul_col0 // tn
        args.append(mul)
        specs.append(pl.BlockSpec((tm, tn), lambda i, j, k: (i, j + off)))
        blk += _nbytes((tm, tn), mul.dtype)
    if add is not None:
        args.append(add)
        specs.append(pl.BlockSpec((tm, tn), lambda i, j, k: (i, j)))
        blk += _nbytes((tm, tn), add.dtype)
    blk += _nbytes((tm, tn), out_dtype)
    scratch = [pltpu.VMEM((tm, tn_w), F32)] if nk > 1 else []
    extra = _nbytes((tm, tn_w), F32) * (4 if nk > 1 else 3)

    return pl.pallas_call(
        functools.partial(_mm_kernel, nk=nk, has_bias=bias is not None, act=act,
                          has_mul=mul is not None, has_add=add is not None),
        grid=grid,
        in_specs=specs,
        out_specs=pl.BlockSpec((tm, tn), lambda i, j, k: (i, j)),
        out_shape=jax.ShapeDtypeStruct((m, n_out), out_dtype),
        scratch_shapes=scratch,
        compiler_params=pltpu.CompilerParams(
            dimension_semantics=("parallel", "parallel", "arbitrary"),
            vmem_limit_bytes=int(min(VMEM_CAP, 2 * blk + extra + 4 * 1024 * 1024))),
        name="matmul_" + (act or "plain"),
    )(*args)


def _attn_kernel(sinks_ref, q_ref, kc_ref, kp_ref, vc_ref, vp_ref,
                 cosc_ref, sinc_ref, cosp_ref, sinp_ref, o_ref, *, nb, n_kv):
    i = pl.program_id(0)
    first = (i % nb) == 0
    blk = WINDOW
    scale = ATT_HEAD ** -0.5

    def rope(t, cos, sin):
        t = t.astype(F32)
        return t * cos + pltpu.roll(t, ATT_HEAD // 2, axis=1) * sin

    cos_c, sin_c = cosc_ref[...], sinc_ref[...]
    cos_p, sin_p = cosp_ref[...], sinp_ref[...]

    rows = ATT_GROUP * blk
    qi = lax.broadcasted_iota(jnp.int32, (rows, 2 * blk), 0) & (blk - 1)
    kj = lax.broadcasted_iota(jnp.int32, (rows, 2 * blk), 1)
    lo = jnp.where(first, blk, 0)
    valid = (kj > qi) & (kj <= qi + blk) & (kj >= lo)
    grp = lax.broadcasted_iota(jnp.int32, (rows, 1), 0) >> (blk.bit_length() - 1)

    for h in range(n_kv):
        cs = slice(h * ATT_HEAD, (h + 1) * ATT_HEAD)
        k_band = jnp.concatenate(
            [rope(kp_ref[:, cs], cos_p, sin_p), rope(kc_ref[:, cs], cos_c, sin_c)],
            axis=0).astype(BF16)
        v_band = jnp.concatenate([vp_ref[:, cs], vc_ref[:, cs]], axis=0)
        q_parts, sink = [], jnp.zeros((rows, 1), F32)
        for g in range(ATT_GROUP):
            hq = h * ATT_GROUP + g
            qs = slice(hq * ATT_HEAD, (hq + 1) * ATT_HEAD)
            q_parts.append((rope(q_ref[:, qs], cos_c, sin_c) * scale).astype(BF16))
            sink = jnp.where(grp == g, sinks_ref[hq], sink)
        q_all = jnp.concatenate(q_parts, axis=0)
        s = lax.dot_general(q_all, k_band, (((1,), (1,)), ((), ())),
                            preferred_element_type=F32)
        s = jnp.where(valid, s, NEG_INF)
        m = jnp.maximum(jnp.max(s, axis=-1, keepdims=True), sink)
        p = jnp.exp(s - m)
        denom = jnp.sum(p, axis=-1, keepdims=True) + jnp.exp(sink - m)
        o = jnp.dot(p.astype(BF16), v_band, preferred_element_type=F32) / denom
        for g in range(ATT_GROUP):
            hq = h * ATT_GROUP + g
            o_ref[:, hq * ATT_HEAD:(hq + 1) * ATT_HEAD] = (
                o[g * blk:(g + 1) * blk].astype(o_ref.dtype))


def _attention(qkv, sinks, cos, sin, *, seq, n_q, n_kv):
    m = qkv.shape[0]
    nb = seq // WINDOW
    q_w, kv_w = n_q * ATT_HEAD, n_kv * ATT_HEAD
    k_blk0, v_blk0 = q_w // kv_w, q_w // kv_w + 1
    prev = lambda i: jnp.maximum(i - 1, 0)
    tab_c = pl.BlockSpec((WINDOW, ATT_HEAD), lambda i: (i % nb, 0))
    tab_p = pl.BlockSpec((WINDOW, ATT_HEAD), lambda i: (jnp.maximum(i % nb - 1, 0), 0))
    blk = (_nbytes((WINDOW, q_w), BF16) * 2 + 4 * _nbytes((WINDOW, kv_w), BF16)
           + 4 * _nbytes((WINDOW, ATT_HEAD), F32))
    return pl.pallas_call(
        functools.partial(_attn_kernel, nb=nb, n_kv=n_kv),
        grid=(m // WINDOW,),
        in_specs=[
            pl.BlockSpec(memory_space=pltpu.SMEM),
            pl.BlockSpec((WINDOW, q_w), lambda i: (i, 0)),
            pl.BlockSpec((WINDOW, kv_w), lambda i: (i, k_blk0)),
            pl.BlockSpec((WINDOW, kv_w), lambda i: (prev(i), k_blk0)),
            pl.BlockSpec((WINDOW, kv_w), lambda i: (i, v_blk0)),
            pl.BlockSpec((WINDOW, kv_w), lambda i: (prev(i), v_blk0)),
            tab_c, tab_c, tab_p, tab_p,
        ],
        out_specs=pl.BlockSpec((WINDOW, q_w), lambda i: (i, 0)),
        out_shape=jax.ShapeDtypeStruct((m, q_w), BF16),
        compiler_params=pltpu.CompilerParams(
            dimension_semantics=("parallel",), vmem_limit_bytes=_vmem_limit(blk)),
        name="swa_attention",
    )(sinks, qkv, qkv, qkv, qkv, qkv, cos, sin, cos, sin)


def _softplus(z):
    return jnp.maximum(z, 0.0) + jnp.log1p(jnp.exp(-jnp.abs(z)))


def _lora_kernel(x_ref, prev_ref, mu_ref, w0_ref, a0_ref, w2_ref, a2_ref, g2_ref,
                 cum_ref, a_ref, g_ref, *, seq, d_decay, d_aaa):
    i = pl.program_id(0)
    tm = x_ref.shape[0]
    x = x_ref[...]
    last_prev = prev_ref[prev_ref.shape[0] - 1:, :]
    last_prev = jnp.where((i * tm) % seq == 0, 0.0, last_prev)
    row = lax.broadcasted_iota(jnp.int32, x.shape, 0)
    prev = jnp.where(row == 0, last_prev, pltpu.roll(x, 1, axis=0))
    xs = x + (prev - x) * mu_ref[...]
    p_w = jnp.tanh(xs[:, :d_decay]).astype(BF16)
    p_a = xs[:, d_decay:d_decay + d_aaa].astype(BF16)
    p_g = jax.nn.sigmoid(xs[:, d_decay + d_aaa:]).astype(BF16)
    w = -_softplus(-(w0_ref[...] + jnp.dot(p_w, w2_ref[...], preferred_element_type=F32))) - 0.5
    log_decay = -jnp.exp(w)
    shift_c = CHUNK.bit_length() - 1
    tr = lax.broadcasted_iota(jnp.int32, (tm, tm), 0)
    tc = lax.broadcasted_iota(jnp.int32, (tm, tm), 1)
    tri_ones = (((tr >> shift_c) == (tc >> shift_c)) & (tr >= tc)).astype(F32).astype(BF16)
    cum_ref[...] = _split_dot(log_decay, tri_ones, 3, ones_left=True)
    a_ref[...] = jax.nn.sigmoid(
        a0_ref[...] + jnp.dot(p_a, a2_ref[...], preferred_element_type=F32))
    g_ref[...] = jnp.dot(p_g, g2_ref[...], preferred_element_type=F32)


def _lora(lora_raw, mu, w0, a0, w2, a2, g2, *, seq, d_decay, d_aaa, tm=256):
    m, wl = lora_raw.shape
    c = w2.shape[1]
    tm = min(tm, seq)
    assert seq % tm == 0 and tm % CHUNK == 0
    sub = SUBLANES
    vec = lambda n: pl.BlockSpec((1, n), lambda i: (0, 0))
    full = lambda a: pl.BlockSpec(a.shape, lambda i: (0, 0))
    out = pl.BlockSpec((tm, c), lambda i: (i, 0))
    blk = (_nbytes((tm, wl), F32) + 3 * _nbytes((tm, c), F32)
           + _nbytes(w2.shape, BF16) + _nbytes(a2.shape, BF16) + _nbytes(g2.shape, BF16))
    return pl.pallas_call(
        functools.partial(_lora_kernel, seq=seq, d_decay=d_decay, d_aaa=d_aaa),
        grid=(m // tm,),
        in_specs=[
            pl.BlockSpec((tm, wl), lambda i: (i, 0)),
            pl.BlockSpec((sub, wl), lambda i: (jnp.maximum(i * (tm // sub) - 1, 0), 0)),
            vec(wl), vec(c), vec(c), full(w2), full(a2), full(g2),
        ],
        out_specs=[out, out, out],
        out_shape=[jax.ShapeDtypeStruct((m, c), F32)] * 3,
        compiler_params=pltpu.CompilerParams(
            dimension_semantics=("parallel",), vmem_limit_bytes=_vmem_limit(blk)),
        name="rwkv_lora",
    )(lora_raw, lora_raw, mu.reshape(1, wl), w0.reshape(1, c), a0.reshape(1, c), w2, a2, g2)


def _bf(x):
    return x.astype(BF16)


def _dot(a, b):
    return jnp.dot(_bf(a), _bf(b), preferred_element_type=F32)


def _dot_nt(a, b):
    return lax.dot_general(_bf(a), _bf(b), (((1,), (1,)), ((), ())), preferred_element_type=F32)


def _dot_tn(a, b):
    return lax.dot_general(_bf(a), _bf(b), (((0,), (0,)), ((), ())), preferred_element_type=F32)


def _split_dot(x, ones_bf, parts, ones_left=False):
    acc, rem = None, x
    for p in range(parts):
        hi = _bf(rem)
        ops = (ones_bf, hi) if ones_left else (hi, ones_bf)
        term = jnp.dot(*ops, preferred_element_type=F32)
        acc = term if acc is None else acc + term
        if p + 1 < parts:
            rem = rem - hi.astype(F32)
    return acc


def _wkv_kernel(r_ref, k_ref, v_ref, cum_ref, a_ref, g_ref,
                mur_ref, muk_ref, muv_ref, kk_ref, ka_ref, rk_ref, lnw_ref, lnb_ref,
                o_ref, rm_scr, yg_scr, bonus_scr, y_scr, h_scr, *, n_chunks, group):
    c_len, two = CHUNK, 2 * CHUNK
    g_rows = group * c_len
    shift_c = CHUNK.bit_length() - 1
    lane = lax.broadcasted_iota(jnp.int32, (c_len, PAIR), 1)
    m0 = (lane < RWKV_HEAD).astype(F32)
    m1 = 1.0 - m0
    row = lax.broadcasted_iota(jnp.int32, (two, two), 0)
    col = lax.broadcasted_iota(jnp.int32, (two, two), 1)
    same = (row >> shift_c) == (col >> shift_c)
    strict = same & ((row & (CHUNK - 1)) > (col & (CHUNK - 1)))
    incl = same & ((row & (CHUNK - 1)) >= (col & (CHUNK - 1)))
    eye = (row == col).astype(F32)
    head_ones = _bf(same.astype(F32))
    g_row = lax.broadcasted_iota(jnp.int32, (g_rows, PAIR), 0)
    row0 = g_row == 0
    chunk_row0 = (g_row & (CHUNK - 1)) == 0

    mu_r, mu_k, mu_v = mur_ref[...], muk_ref[...], muv_ref[...]
    k_k, k_a, r_k = kk_ref[...], ka_ref[...], rk_ref[...]

    def stack(x):
        return jnp.concatenate([x * m0, x * m1], axis=0)

    rm_scr[pl.ds(0, group * 2 * two), :] = jnp.zeros((group * 2 * two, PAIR), BF16)
    yg_scr[pl.ds(0, group * 2 * two), :] = jnp.zeros((group * 2 * two, PAIR), F32)
    h_scr[...] = jnp.zeros_like(h_scr)

    def serial_step(slot, h):
        base = pl.multiple_of(slot * (2 * two), 2 * two)
        out = jnp.dot(rm_scr[pl.ds(base, 2 * two), :], h,
                      preferred_element_type=F32) + yg_scr[pl.ds(base, 2 * two), :]
        rows = pl.ds(pl.multiple_of(slot * c_len, c_len), c_len)
        y_scr[rows, :] = out[:c_len] + out[c_len:two]
        return _bf(out[two:])

    n_boundaries = 9
    serial_at = [j * n_boundaries // group for j in range(group)]

    def local(i, _):
        rows = pl.ds(pl.multiple_of(i * g_rows, g_rows), g_rows)
        before = pl.ds(pl.multiple_of(jnp.maximum(i * g_rows - SUBLANES, 0), SUBLANES), SUBLANES)
        keep_prev = jnp.where(i > 0, 1.0, 0.0)
        state = [h_scr[...]]

        def boundary(k):
            for j in range(group):
                if serial_at[j] == k:
                    state[0] = serial_step(i * group + j, state[0])

        def shift(ref, mu):
            x = ref[rows, :]
            last = ref[before, :][SUBLANES - 1:, :] * keep_prev
            prev = jnp.where(row0, last, pltpu.roll(x, 1, axis=0))
            return x + (prev - x) * mu

        rs, ks, vs = shift(r_ref, mu_r), shift(k_ref, mu_k), shift(v_ref, mu_v)
        cum, a = cum_ref[rows, :], a_ref[rows, :]

        kk = ks * k_k
        norm = jnp.sqrt(_split_dot(kk * kk, head_ones, 2))
        kk = kk / jnp.maximum(norm, 1e-12)
        kx = ks * (1.0 + (a - 1.0) * k_a)
        b = kk * a
        bonus_scr[rows, :] = _split_dot(rs * kx * r_k, head_ones, 2) * vs

        cum_before = jnp.where(chunk_row0, 0.0, pltpu.roll(cum, 1, axis=0))
        e_neg = jnp.exp(-cum)
        at_g = -kk * jnp.exp(cum_before)
        rt_g = rs * jnp.exp(cum)
        bt_g, kt_g = _bf(b * e_neg), _bf(kx * e_neg)

        st = []
        for j in range(group):
            sl = slice(j * c_len, (j + 1) * c_len)
            c_end = cum[(j + 1) * c_len - 1:(j + 1) * c_len, :]
            e_rem = jnp.exp(c_end - cum[sl])
            at2, rt2 = stack(at_g[sl]), stack(rt_g[sl])
            st.append(dict(
                at2=at2, rt2=rt2, v2=_bf(stack(vs[sl])), d_end=jnp.exp(c_end),
                bk2=jnp.concatenate([stack(b[sl] * e_rem), stack(kx[sl] * e_rem)], axis=0),
                s=_dot_nt(jnp.concatenate([at2, rt2], axis=0),
                          jnp.concatenate([bt_g[sl], bt_g[sl], kt_g[sl], kt_g[sl]], axis=0))))
        boundary(0)
        for d in st:
            s = d.pop("s")
            x = jnp.where(strict, s[:two, :two], 0.0)
            a_ak = jnp.where(strict, s[:two, two:], 0.0)
            d["rbk"] = jnp.concatenate([jnp.where(incl, s[two:, :two], 0.0),
                                        jnp.where(incl, s[two:, two:], 0.0)], axis=1)
            d["t_sum"] = eye + x
            d["x_pow"] = _dot(x, x)
            d["z2"] = _dot(a_ak, d["v2"])
        boundary(1)
        for step in range(CHUNK.bit_length() - 3):
            for d in st:
                both = _dot(d["x_pow"], jnp.concatenate([d["t_sum"], d["x_pow"]], axis=1))
                d["t_sum"] = d["t_sum"] + both[:, :two]
                d["x_pow"] = both[:, two:]
            boundary(2 + step)
        for d in st:
            d["t_inv"] = d["t_sum"] + _dot(d["x_pow"], d["t_sum"])
        boundary(6)
        for d in st:
            pq = _dot(d["t_inv"], jnp.concatenate([_bf(d["at2"]), _bf(d["z2"])], axis=1))
            d["rhs"] = jnp.concatenate(
                [_bf(pq), jnp.concatenate([jnp.zeros_like(d["v2"]), d["v2"]], axis=1)], axis=0)
        boundary(7)
        for d in st:
            d["n1"] = _dot(d["rbk"], d["rhs"])
            d["n2"] = _dot_tn(d["bk2"], d["rhs"])
        boundary(8)
        for j, d in enumerate(st):
            n1, n2 = d["n1"], d["n2"]
            rp2 = d["rt2"] + n1[:, :two]
            m_blk = n2[:, :two] + eye * d["d_end"]
            base = pl.multiple_of(((i + 1) * group + j) * (2 * two), 2 * two)
            rm_scr[pl.ds(base, 2 * two), :] = _bf(jnp.concatenate([rp2, m_blk], axis=0))
            yg_scr[pl.ds(base, 2 * two), :] = jnp.concatenate([n1[:, two:], n2[:, two:]], axis=0)
        h_scr[...] = state[0]
        return 0

    lax.fori_loop(0, n_chunks // group, local, 0)

    def drain(j, _):
        h_scr[...] = serial_step(n_chunks + j, h_scr[...])
        return 0

    lax.fori_loop(0, group, drain, 0)

    ln_w, ln_b = lnw_ref[...], lnb_ref[...]
    inv_n = 1.0 / RWKV_HEAD

    def finish(i, _):
        rows = pl.ds(pl.multiple_of(i * g_rows, g_rows), g_rows)
        y = y_scr[pl.ds(pl.multiple_of((i + 1) * g_rows, g_rows), g_rows), :]
        mean = _split_dot(y, head_ones, 2) * inv_n
        d = y - mean
        var = _split_dot(d * d, head_ones, 2) * inv_n
        yn = d * lax.rsqrt(var + GN_EPS) * ln_w + ln_b
        o_ref[rows, :] = ((yn + bonus_scr[rows, :]) * g_ref[rows, :]).astype(o_ref.dtype)
        return 0

    lax.fori_loop(0, n_chunks // group, finish, 0)


def _wkv(rkv, cum, a, g, mu_rkv, k_k, k_a, r_k, ln_w, ln_b, *, batch, seq, group=8):
    m, c = cum.shape
    n_pairs = c // PAIR
    n_chunks = seq // CHUNK
    assert seq % (group * CHUNK) == 0
    col = lambda off: pl.BlockSpec((seq, PAIR), lambda b, p: (b, p + off))
    vec = lambda off: pl.BlockSpec((1, PAIR), lambda b, p: (0, p + off))
    vecs = [x.reshape(1, c) for x in (k_k, k_a, r_k, ln_w, ln_b)]
    blk = 6 * _nbytes((seq, PAIR), F32) + _nbytes((seq, PAIR), BF16)
    slot_rows = (n_chunks + group) * 4 * CHUNK
    y_rows = seq + group * CHUNK
    scratch = [pltpu.VMEM((slot_rows, PAIR), BF16),
               pltpu.VMEM((slot_rows, PAIR), F32),
               pltpu.VMEM((seq, PAIR), F32),
               pltpu.VMEM((y_rows, PAIR), F32),
               pltpu.VMEM((PAIR, PAIR), BF16)]
    scr = (_nbytes((slot_rows, PAIR), BF16) + _nbytes((slot_rows, PAIR), F32)
           + _nbytes((seq, PAIR), F32) + _nbytes((y_rows, PAIR), F32))
    return pl.pallas_call(
        functools.partial(_wkv_kernel, n_chunks=n_chunks, group=group),
        grid=(batch, n_pairs),
        in_specs=[col(0), col(n_pairs), col(2 * n_pairs), col(0), col(0), col(0),
                  vec(0), vec(n_pairs), vec(2 * n_pairs)] + [vec(0)] * 5,
        out_specs=col(0),
        out_shape=jax.ShapeDtypeStruct((m, c), BF16),
        scratch_shapes=scratch,
        compiler_params=pltpu.CompilerParams(
            dimension_semantics=("parallel", "parallel"),
            vmem_limit_bytes=int(min(VMEM_CAP, 2 * blk + scr + 8 * 1024 * 1024))),
        name="wkv7_chunked",
    )(rkv, rkv, rkv, cum, a, g, mu_rkv, mu_rkv, mu_rkv, *vecs)


def _rope_tables(seq):
    pos = jnp.arange(seq, dtype=F32)
    inv_freq = ROPE_THETA ** (-jnp.arange(0, ATT_HEAD, 2, dtype=F32) / ATT_HEAD)
    ang = pos[:, None] * inv_freq[None, :]
    cos, sin = jnp.cos(ang), jnp.sin(ang)
    return jnp.concatenate([cos, cos], axis=-1), jnp.concatenate([-sin, sin], axis=-1)


def _pad_to(x, n, axis):
    pad = n - x.shape[axis]
    if pad == 0:
        return x
    widths = [(0, 0)] * x.ndim
    widths[axis] = (0, pad)
    return jnp.pad(x, widths)


def _round_up(n, mult):
    return -(-n // mult) * mult


def _layer(x2, batch, seq, norm_mix_pre, norm_mix_post, norm_ffn_pre, norm_ffn_post, w_in, b_qkv,
           att_sinks, mu_shift, w0, w2, a0, a2, g2, k_k, k_a, r_k, ln_x_w, ln_x_b,
           w_att_branch, w_rwkv_branch, w_out, w_ffn_gate, w_ffn_up, w_ffn_down):
    d = x2.shape[1]
    n_q = att_sinks.shape[0]
    q_w = n_q * ATT_HEAD
    qkv_w = b_qkv.shape[0]
    n_kv = (qkv_w - q_w) // (2 * ATT_HEAD)
    c = w0.shape[0]
    d_decay, d_aaa, d_gate = w2.shape[0], a2.shape[0], g2.shape[0]
    lora_w = d_decay + d_aaa + d_gate
    lora_pad = _round_up(lora_w, LANES)
    ffn = w_ffn_gate.shape[1]

    c0 = qkv_w
    w_att = _bf(w_in[:, :c0])
    w_rkv = _bf(w_in[:, c0:c0 + 3 * c])
    w_lora = _bf(_pad_to(w_in[:, c0 + 3 * c:c0 + 3 * c + lora_w], lora_pad, 1))
    w_gates = _bf(w_in[:, c0 + 3 * c + lora_w:])
    mu_rkv = mu_shift[:3 * c].reshape(1, 3 * c)
    mu_lora = _pad_to(mu_shift[3 * c:], lora_pad, 0)
    g2_pad = _bf(_pad_to(g2, lora_pad - d_decay - d_aaa, 0))

    (h,) = _norm_call(x2, norm_mix_pre, emit_main=False)

    qkv = _matmul(h, w_att, out_dtype=BF16, bias=b_qkv)
    rkv = _matmul(h, w_rkv, out_dtype=F32)
    lora_raw = _matmul(h, w_lora, out_dtype=F32, tn=lora_pad)
    gates = _matmul(h, w_gates, out_dtype=BF16, act="sigmoid")

    cos, sin = _rope_tables(seq)
    o_att = _attention(qkv, att_sinks.astype(F32), cos, sin, seq=seq, n_q=n_q, n_kv=n_kv)

    cum, a_gate, g_out = _lora(lora_raw, mu_lora, w0, a0, _bf(w2), _bf(a2), g2_pad,
                              seq=seq, d_decay=d_decay, d_aaa=d_aaa)
    o_rwkv = _wkv(rkv, cum, a_gate, g_out, mu_rkv, k_k, k_a, r_k.reshape(-1), ln_x_w, ln_x_b,
                  batch=batch, seq=seq)

    m_att = _matmul(o_att, _bf(w_att_branch), out_dtype=F32, tn=512, mul=gates, mul_col0=0)
    merged = _matmul(o_rwkv, _bf(w_rwkv_branch), out_dtype=BF16, tn=512, mul=gates, mul_col0=d,
                     add=m_att)
    mixed = _matmul(merged, _bf(w_out), out_dtype=F32)
    x1, h2 = _norm_call(mixed, norm_mix_post, res=x2, g_next=norm_ffn_pre)

    tn_ffn = 512
    ffn_pad = _round_up(ffn, 2 * tn_ffn)
    nt = ffn_pad // tn_ffn
    wg = _bf(_pad_to(w_ffn_gate, ffn_pad, 1)).reshape(d, nt, tn_ffn)
    wu = _bf(_pad_to(w_ffn_up, ffn_pad, 1)).reshape(d, nt, tn_ffn)
    w_gu = jnp.concatenate([wg, wu], axis=2).reshape(d, 2 * ffn_pad)
    w_dn = _bf(_pad_to(w_ffn_down, ffn_pad, 0))
    act = _matmul(h2, w_gu, out_dtype=BF16, act="swiglu", tn=tn_ffn)
    f = _matmul(act, w_dn, out_dtype=F32, tk=ffn_pad // 4)
    (out,) = _norm_call(f, norm_ffn_post, res=x1)
    return out


def kernel(x, norm_mix_pre, norm_mix_post, norm_ffn_pre, norm_ffn_post, w_in, b_qkv, att_sinks,
           mu_shift, w0, w2, a0, a2, g2, k_k, k_a, r_k, ln_x_w, ln_x_b,
           w_att_branch, w_rwkv_branch, w_out, w_ffn_gate, w_ffn_up, w_ffn_down):
    batch, seq, d = x.shape
    x2 = x.reshape(batch * seq, d)
    for l in range(w_in.shape[0]):
        x2 = _layer(x2, batch, seq, norm_mix_pre[l], norm_mix_post[l], norm_ffn_pre[l],
                    norm_ffn_post[l], w_in[l], b_qkv[l], att_sinks[l], mu_shift[l], w0[l], w2[l],
                    a0[l], a2[l], g2[l], k_k[l], k_a[l], r_k[l], ln_x_w[l], ln_x_b[l],
                    w_att_branch[l], w_rwkv_branch[l], w_out[l],
                    w_ffn_gate[l], w_ffn_up[l], w_ffn_down[l])
    return x2.reshape(batch, seq, d)
```

```python
import functools

import jax
import jax.numpy as jnp
from jax import lax
from jax.experimental import pallas as pl
from jax.experimental.pallas import tpu as pltpu

F32 = jnp.float32
BF16 = jnp.bfloat16

LANES = 128
SUBLANES = 8
V7X_VMEM_BYTES = 64 * 1024 * 1024
VMEM_CAP = V7X_VMEM_BYTES - 6 * 1024 * 1024

ATT_HEAD = 128
ATT_GROUP = 4
WINDOW = 128
ROPE_THETA = 10000.0
NEG_INF = -1e30
RWKV_HEAD = 64
CHUNK = 64
PAIR = 2 * RWKV_HEAD
GN_EPS = 64e-5
RMS_EPS = 1e-6


def _vmem_limit(block_bytes):
    return int(min(VMEM_CAP, 2 * block_bytes + 16 * 1024 * 1024))


def _nbytes(shape, dtype):
    n = 1
    for s in shape:
        n *= s
    return n * jnp.dtype(dtype).itemsize


def _rms(x, g):
    return x * lax.rsqrt(jnp.mean(x * x, axis=-1, keepdims=True) + RMS_EPS) * g


def _norm_kernel(*refs, has_res, emit_main, second_norm):
    it = iter(refs)
    x_ref, g_ref = next(it), next(it)
    res_ref = next(it) if has_res else None
    g2_ref = next(it) if second_norm else None
    main_ref = next(it) if emit_main else None
    next_ref = next(it) if (second_norm or not emit_main) else None
    y = _rms(x_ref[...], g_ref[...])
    if has_res:
        y = res_ref[...] + y
    if emit_main:
        main_ref[...] = y
    if second_norm:
        next_ref[...] = _rms(y, g2_ref[...]).astype(next_ref.dtype)
    elif not emit_main:
        next_ref[...] = y.astype(next_ref.dtype)


def _norm_call(x, g, res=None, g_next=None, *, emit_main=True, tm=256):
    m, d = x.shape
    tm = min(tm, m)
    second_norm = g_next is not None
    assert emit_main or not second_norm
    row = pl.BlockSpec((tm, d), lambda i: (i, 0))
    vec = pl.BlockSpec((1, d), lambda i: (0, 0))
    args, specs = [x, g.reshape(1, d)], [row, vec]
    if res is not None:
        args.append(res); specs.append(row)
    if second_norm:
        args.append(g_next.reshape(1, d)); specs.append(vec)
    out_shape, out_specs = [], []
    if emit_main:
        out_shape.append(jax.ShapeDtypeStruct((m, d), F32)); out_specs.append(row)
    if second_norm or not emit_main:
        out_shape.append(jax.ShapeDtypeStruct((m, d), BF16)); out_specs.append(row)
    blk = _nbytes((tm, d), F32) * (len(args) + len(out_shape))
    outs = pl.pallas_call(
        functools.partial(_norm_kernel, has_res=res is not None, emit_main=emit_main,
                          second_norm=second_norm),
        grid=(m // tm,),
        in_specs=specs, out_specs=out_specs, out_shape=out_shape,
        compiler_params=pltpu.CompilerParams(
            dimension_semantics=("parallel",), vmem_limit_bytes=_vmem_limit(blk)),
        name="rmsnorm",
    )(*args)
    return outs


def _mm_kernel(*refs, nk, has_bias, act, has_mul, has_add):
    it = iter(refs)
    a_ref, w_ref = next(it), next(it)
    up_ref = next(it) if act == "swiglu" else None
    bias_ref = next(it) if has_bias else None
    mul_ref = next(it) if has_mul else None
    add_ref = next(it) if has_add else None
    o_ref = next(it)
    acc_ref = next(it) if nk > 1 else None

    part = jnp.dot(a_ref[...], w_ref[...], preferred_element_type=F32)

    def finish(y):
        if has_bias:
            y = y + bias_ref[...]
        if act == "sigmoid":
            y = jax.nn.sigmoid(y)
        elif act == "swiglu":
            up = jnp.dot(a_ref[...], up_ref[...], preferred_element_type=F32)
            y = y * jax.nn.sigmoid(y) * up
        if has_mul:
            y = y * mul_ref[...].astype(F32)
        if has_add:
            y = y + add_ref[...].astype(F32)
        o_ref[...] = y.astype(o_ref.dtype)

    if nk == 1:
        finish(part)
    else:
        k = pl.program_id(2)

        @pl.when(k == 0)
        def _():
            acc_ref[...] = part

        @pl.when(jnp.logical_and(k > 0, k < nk - 1))
        def _():
            acc_ref[...] += part

        @pl.when(k == nk - 1)
        def _():
            finish(acc_ref[...] + part)


def _matmul(a, w, *, out_dtype, tm=1024, tn=1024, tk=None, bias=None, act=None, w_up=None,
            mul=None, mul_col0=0, add=None):
    m, kdim = a.shape
    n_out = w.shape[1]
    tm = min(tm, m)
    tk = kdim if tk is None else tk
    nk = kdim // tk
    assert m % tm == 0 and kdim % tk == 0
    assert n_out % tn == 0 and tn % LANES == 0
    assert (act == "swiglu") == (w_up is not None) and (w_up is None or nk == 1)
    grid = (m // tm, n_out // tn, nk)

    args = [a, w]
    w_spec = pl.BlockSpec((tk, tn), lambda i, j, k: (k, j))
    specs = [pl.BlockSpec((tm, tk), lambda i, j, k: (i, k)), w_spec]
    blk = _nbytes((tm, tk), a.dtype) + _nbytes((tk, tn), w.dtype)
    if w_up is not None:
        args.append(w_up)
        specs.append(w_spec)
        blk += _nbytes((tk, tn), w_up.dtype)
    if bias is not None:
        args.append(bias.reshape(1, n_out).astype(F32))
        specs.append(pl.BlockSpec((1, tn), lambda i, j, k: (0, j)))
    if mul is not None:
        assert mul_col0 % tn == 0
        off = mul_col0 // tn
        args.append(mul)
        specs.append(pl.BlockSpec((tm, tn), lambda i, j, k: (i, j + off)))
        blk += _nbytes((tm, tn), mul.dtype)
    if add is not None:
        args.append(add)
        specs.append(pl.BlockSpec((tm, tn), lambda i, j, k: (i, j)))
        blk += _nbytes((tm, tn), add.dtype)
    blk += _nbytes((tm, tn), out_dtype)
    scratch = [pltpu.VMEM((tm, tn), F32)] if nk > 1 else []
    extra = _nbytes((tm, tn), F32) * (4 if (nk > 1 or w_up is not None) else 3)

    return pl.pallas_call(
        functools.partial(_mm_kernel, nk=nk, has_bias=bias is not None, act=act,
                          has_mul=mul is not None, has_add=add is not None),
        grid=grid,
        in_specs=specs,
        out_specs=pl.BlockSpec((tm, tn), lambda i, j, k: (i, j)),
        out_shape=jax.ShapeDtypeStruct((m, n_out), out_dtype),
        scratch_shapes=scratch,
        compiler_params=pltpu.CompilerParams(
            dimension_semantics=("parallel", "parallel", "arbitrary"),
            vmem_limit_bytes=int(min(VMEM_CAP, 2 * blk + extra + 4 * 1024 * 1024))),
        name="matmul_" + (act or "plain"),
    )(*args)


def _attn_kernel(sinks_ref, q_ref, kc_ref, kp_ref, vc_ref, vp_ref,
                 cosq_ref, sinq_ref, cosc_ref, sinc_ref, cosp_ref, sinp_ref, o_ref, bias_scr,
                 *, nb, n_kv):
    i = pl.program_id(0)
    first = (i % nb) == 0
    blk = WINDOW

    def rope(t, cos, sin):
        t = t.astype(F32)
        return t * cos + pltpu.roll(t, ATT_HEAD // 2, axis=1) * sin

    cos_q, sin_q = cosq_ref[...], sinq_ref[...]
    cos_c, sin_c = cosc_ref[...], sinc_ref[...]
    cos_p, sin_p = cosp_ref[...], sinp_ref[...]

    qi = lax.broadcasted_iota(jnp.int32, (blk, 2 * blk), 0)
    kj = lax.broadcasted_iota(jnp.int32, (blk, 2 * blk), 1)
    lo = jnp.where(first, blk, 0)
    valid = (kj > qi) & (kj <= qi + blk) & (kj >= lo)
    bias_scr[...] = jnp.where(valid, 0.0, NEG_INF)
    ones = jnp.ones((2 * blk, ATT_HEAD), BF16)

    def scores(h):
        cs = slice(h * ATT_HEAD, (h + 1) * ATT_HEAD)
        k_band = jnp.concatenate(
            [rope(kp_ref[:, cs], cos_p, sin_p), rope(kc_ref[:, cs], cos_c, sin_c)],
            axis=0).astype(BF16)
        q_all = jnp.concatenate(
            [rope(q_ref[:, hq * ATT_HEAD:(hq + 1) * ATT_HEAD], cos_q, sin_q).astype(BF16)
             for hq in range(h * ATT_GROUP, (h + 1) * ATT_GROUP)], axis=0)
        return lax.dot_general(q_all, k_band, (((1,), (1,)), ((), ())),
                               preferred_element_type=F32)

    def finalize(h, pv, sink_parts):
        for g in range(ATT_GROUP):
            hq = h * ATT_GROUP + g
            pv_g = pv[g * blk:(g + 1) * blk]
            o_g = pv_g[:, :ATT_HEAD] / (pv_g[:, ATT_HEAD:] + sink_parts[g])
            o_ref[:, hq * ATT_HEAD:(hq + 1) * ATT_HEAD] = o_g.astype(o_ref.dtype)

    s_next, pending = scores(0), None
    for h in range(n_kv):
        s = s_next
        if h + 1 < n_kv:
            s_next = scores(h + 1)
        cs = slice(h * ATT_HEAD, (h + 1) * ATT_HEAD)
        v_ext = jnp.concatenate(
            [jnp.concatenate([vp_ref[:, cs], vc_ref[:, cs]], axis=0), ones], axis=1)
        p_parts, sink_parts = [], []
        for g in range(ATT_GROUP):
            sink = sinks_ref[h * ATT_GROUP + g]
            s_g = s[g * blk:(g + 1) * blk] + bias_scr[...]
            m_g = jnp.maximum(jnp.max(s_g, axis=-1, keepdims=True), sink)
            p_parts.append(jnp.exp(s_g - m_g).astype(BF16))
            sink_parts.append(jnp.exp(sink - m_g))
        pv = jnp.dot(jnp.concatenate(p_parts, axis=0), v_ext,
                     preferred_element_type=F32)
        if pending is not None:
            finalize(*pending)
        pending = (h, pv, sink_parts)
    finalize(*pending)


def _attention(qkv, sinks, cos, sin, *, seq, n_q, n_kv):
    m = qkv.shape[0]
    nb = seq // WINDOW
    q_w, kv_w = n_q * ATT_HEAD, n_kv * ATT_HEAD
    k_blk0, v_blk0 = q_w // kv_w, q_w // kv_w + 1
    prev = lambda i: jnp.maximum(i - 1, 0)
    scale = ATT_HEAD ** -0.5
    tab_c =pl.BlockSpec((WINDOW, ATT_HEAD), lambda i: (i % nb, 0))
    tab_p = pl.BlockSpec((WINDOW, ATT_HEAD), lambda i: (jnp.maximum(i % nb - 1, 0), 0))
    blk = (_nbytes((WINDOW, q_w), BF16) * 2 + 4 * _nbytes((WINDOW, kv_w), BF16)
           + 4 * _nbytes((WINDOW, ATT_HEAD), F32))
    return pl.pallas_call(
        functools.partial(_attn_kernel, nb=nb, n_kv=n_kv),
        grid=(m // WINDOW,),
        in_specs=[
            pl.BlockSpec(memory_space=pltpu.SMEM),
            pl.BlockSpec((WINDOW, q_w), lambda i: (i, 0)),
            pl.BlockSpec((WINDOW, kv_w), lambda i: (i, k_blk0)),
            pl.BlockSpec((WINDOW, kv_w), lambda i: (prev(i), k_blk0)),
            pl.BlockSpec((WINDOW, kv_w), lambda i: (i, v_blk0)),
            pl.BlockSpec((WINDOW, kv_w), lambda i: (prev(i), v_blk0)),
            tab_c, tab_c, tab_c, tab_c, tab_p, tab_p,
        ],
        out_specs=pl.BlockSpec((WINDOW, q_w), lambda i: (i, 0)),
        out_shape=jax.ShapeDtypeStruct((m, q_w), BF16),
        scratch_shapes=[pltpu.VMEM((WINDOW, 2 * WINDOW), F32)],
        compiler_params=pltpu.CompilerParams(
            dimension_semantics=("parallel",), vmem_limit_bytes=_vmem_limit(blk)),
        name="swa_attention",
    )(sinks, qkv, qkv, qkv, qkv, qkv, cos * scale, sin * scale, cos, sin, cos, sin)


def _softplus(z):
    return jnp.maximum(z, 0.0) + jnp.log1p(jnp.exp(-jnp.abs(z)))


def _lora_kernel(x_ref, prev_ref, mu_ref, w0_ref, a0_ref, w2_ref, a2_ref, g2_ref,
                 cum_ref, a_ref, g_ref, *, seq, d_decay, d_aaa):
    i = pl.program_id(0)
    tm = x_ref.shape[0]
    x = x_ref[...]
    last_prev = prev_ref[prev_ref.shape[0] - 1:, :]
    last_prev = jnp.where((i * tm) % seq == 0, 0.0, last_prev)
    row = lax.broadcasted_iota(jnp.int32, x.shape, 0)
    prev = jnp.where(row == 0, last_prev, pltpu.roll(x, 1, axis=0))
    xs = x + (prev - x) * mu_ref[...]
    p_w = jnp.tanh(xs[:, :d_decay]).astype(BF16)
    p_a = xs[:, d_decay:d_decay + d_aaa].astype(BF16)
    p_g = jax.nn.sigmoid(xs[:, d_decay + d_aaa:]).astype(BF16)
    w = -_softplus(-(w0_ref[...] + jnp.dot(p_w, w2_ref[...], preferred_element_type=F32))) - 0.5
    log_decay = -jnp.exp(w)
    shift_c = CHUNK.bit_length() - 1
    tr = lax.broadcasted_iota(jnp.int32, (tm, tm), 0)
    tc = lax.broadcasted_iota(jnp.int32, (tm, tm), 1)
    tri_ones = (((tr >> shift_c) == (tc >> shift_c)) & (tr >= tc)).astype(F32).astype(BF16)
    cum_ref[...] = _split_dot(log_decay, tri_ones, 3, ones_left=True)
    a_ref[...] = jax.nn.sigmoid(
        a0_ref[...] + jnp.dot(p_a, a2_ref[...], preferred_element_type=F32))
    g_ref[...] = jnp.dot(p_g, g2_ref[...], preferred_element_type=F32)


def _lora(lora_raw, mu, w0, a0, w2, a2, g2, *, seq, d_decay, d_aaa, tm=256):
    m, wl = lora_raw.shape
    c = w2.shape[1]
    tm = min(tm, seq)
    assert seq % tm == 0 and tm % CHUNK == 0
    sub = SUBLANES
    vec = lambda n: pl.BlockSpec((1, n), lambda i: (0, 0))
    full = lambda a: pl.BlockSpec(a.shape, lambda i: (0, 0))
    out = pl.BlockSpec((tm, c), lambda i: (i, 0))
    blk = (_nbytes((tm, wl), F32) + 3 * _nbytes((tm, c), F32)
           + _nbytes(w2.shape, BF16) + _nbytes(a2.shape, BF16) + _nbytes(g2.shape, BF16))
    return pl.pallas_call(
        functools.partial(_lora_kernel, seq=seq, d_decay=d_decay, d_aaa=d_aaa),
        grid=(m // tm,),
        in_specs=[
            pl.BlockSpec((tm, wl), lambda i: (i, 0)),
            pl.BlockSpec((sub, wl), lambda i: (jnp.maximum(i * (tm // sub) - 1, 0), 0)),
            vec(wl), vec(c), vec(c), full(w2), full(a2), full(g2),
        ],
        out_specs=[out, out, out],
        out_shape=[jax.ShapeDtypeStruct((m, c), F32)] * 3,
        compiler_params=pltpu.CompilerParams(
            dimension_semantics=("parallel",), vmem_limit_bytes=_vmem_limit(blk)),
        name="rwkv_lora",
    )(lora_raw, lora_raw, mu.reshape(1, wl), w0.reshape(1, c), a0.reshape(1, c), w2, a2, g2)


def _bf(x):
    return x.astype(BF16)


def _dot(a, b):
    return jnp.dot(_bf(a), _bf(b), preferred_element_type=F32)


def _dot_nt(a, b):
    return lax.dot_general(_bf(a), _bf(b), (((1,), (1,)), ((), ())), preferred_element_type=F32)


def _dot_tn(a, b):
    return lax.dot_general(_bf(a), _bf(b), (((0,), (0,)), ((), ())), preferred_element_type=F32)


def _split_dot(x, ones_bf, parts, ones_left=False):
    acc, rem = None, x
    for p in range(parts):
        hi = _bf(rem)
        ops = (ones_bf, hi) if ones_left else (hi, ones_bf)
        term = jnp.dot(*ops, preferred_element_type=F32)
        acc = term if acc is None else acc + term
        if p + 1 < parts:
            rem = rem - hi.astype(F32)
    return acc


def _wkv_kernel(r_ref, k_ref, v_ref, cum_ref, a_ref, g_ref,
                mur_ref, muk_ref, muv_ref, kk_ref, ka_ref, rk_ref, lnw_ref, lnb_ref,
                o_ref, rm_scr, yg_scr, bonus_scr, y_scr, h_scr, *, n_chunks, group):
    c_len, two = CHUNK, 2 * CHUNK
    g_rows = group * c_len
    shift_c = CHUNK.bit_length() - 1
    lane = lax.broadcasted_iota(jnp.int32, (c_len, PAIR), 1)
    m0 = (lane < RWKV_HEAD).astype(F32)
    m1 = 1.0 - m0
    row = lax.broadcasted_iota(jnp.int32, (two, two), 0)
    col = lax.broadcasted_iota(jnp.int32, (two, two), 1)
    same = (row >> shift_c) == (col >> shift_c)
    strict = same & ((row & (CHUNK - 1)) > (col & (CHUNK - 1)))
    incl = same & ((row & (CHUNK - 1)) >= (col & (CHUNK - 1)))
    eye = (row == col).astype(F32)
    head_ones = _bf(same.astype(F32))
    g_row = lax.broadcasted_iota(jnp.int32, (g_rows, PAIR), 0)
    row0 = g_row == 0
    chunk_row0 = (g_row & (CHUNK - 1)) == 0

    mu_r, mu_k, mu_v = mur_ref[...], muk_ref[...], muv_ref[...]
    k_k, k_a, r_k = kk_ref[...], ka_ref[...], rk_ref[...]

    def stack(x):
        return jnp.concatenate([x * m0, x * m1], axis=0)

    rm_scr[pl.ds(0, group * 2 * two), :] = jnp.zeros((group * 2 * two, PAIR), BF16)
    yg_scr[pl.ds(0, group * 2 * two), :] = jnp.zeros((group * 2 * two, PAIR), F32)
    h_scr[...] = jnp.zeros_like(h_scr)

    def serial_step(slot, h):
        base = pl.multiple_of(slot * (2 * two), 2 * two)
        out = jnp.dot(rm_scr[pl.ds(base, 2 * two), :], h,
                      preferred_element_type=F32) + yg_scr[pl.ds(base, 2 * two), :]
        rows = pl.ds(pl.multiple_of(slot * c_len, c_len), c_len)
        y_scr[rows, :] = out[:c_len] + out[c_len:two]
        return _bf(out[two:])

    n_boundaries = 9
    serial_at = [j * n_boundaries // group for j in range(group)]

    def local(i, _):
        rows = pl.ds(pl.multiple_of(i * g_rows, g_rows), g_rows)
        before = pl.ds(pl.multiple_of(jnp.maximum(i * g_rows - SUBLANES, 0), SUBLANES), SUBLANES)
        keep_prev = jnp.where(i > 0, 1.0, 0.0)
        state = [h_scr[...]]

        def boundary(k):
            for j in range(group):
                if serial_at[j] == k:
                    state[0] = serial_step(i * group + j, state[0])

        def shift(ref, mu):
            x = ref[rows, :]
            last = ref[before, :][SUBLANES - 1:, :] * keep_prev
            prev = jnp.where(row0, last, pltpu.roll(x, 1, axis=0))
            return x + (prev - x) * mu

        rs, ks, vs = shift(r_ref, mu_r), shift(k_ref, mu_k), shift(v_ref, mu_v)
        cum, a = cum_ref[rows, :], a_ref[rows, :]

        kk = ks * k_k
        norm = jnp.sqrt(_split_dot(kk * kk, head_ones, 2))
        kk = kk / jnp.maximum(norm, 1e-12)
        kx = ks * (1.0 + (a - 1.0) * k_a)
        b = kk * a
        bonus_scr[rows, :] = _split_dot(rs * kx * r_k, head_ones, 2) * vs

        cum_before = jnp.where(chunk_row0, 0.0, pltpu.roll(cum, 1, axis=0))
        e_neg = jnp.exp(-cum)
        at_g = -kk * jnp.exp(cum_before)
        rt_g = rs * jnp.exp(cum)
        bt_g, kt_g = _bf(b * e_neg), _bf(kx * e_neg)

        st = []
        for j in range(group):
            sl = slice(j * c_len, (j + 1) * c_len)
            c_end = cum[(j + 1) * c_len - 1:(j + 1) * c_len, :]
            e_rem = jnp.exp(c_end - cum[sl])
            at2, rt2 = stack(at_g[sl]), stack(rt_g[sl])
            st.append(dict(
                at2=at2, rt2=rt2, v2=_bf(stack(vs[sl])), d_end=jnp.exp(c_end),
                bk2=jnp.concatenate([stack(b[sl] * e_rem), stack(kx[sl] * e_rem)], axis=0),
                s=_dot_nt(jnp.concatenate([at2, rt2], axis=0),
                          jnp.concatenate([bt_g[sl], bt_g[sl], kt_g[sl], kt_g[sl]], axis=0))))
        boundary(0)
        for d in st:
            s = d.pop("s")
            x = jnp.where(strict, s[:two, :two], 0.0)
            a_ak = jnp.where(strict, s[:two, two:], 0.0)
            d["rbk"] = jnp.concatenate([jnp.where(incl, s[two:, :two], 0.0),
                                        jnp.where(incl, s[two:, two:], 0.0)], axis=1)
            d["t_sum"] = eye + x
            d["x_pow"] = _dot(x, x)
            d["z2"] = _dot(a_ak, d["v2"])
        boundary(1)
        for step in range(CHUNK.bit_length() - 3):
            for d in st:
                both = _dot(d["x_pow"], jnp.concatenate([d["t_sum"], d["x_pow"]], axis=1))
                d["t_sum"] = d["t_sum"] + both[:, :two]
                d["x_pow"] = both[:, two:]
            boundary(2 + step)
        for d in st:
            d["t_inv"] = d["t_sum"] + _dot(d["x_pow"], d["t_sum"])
        boundary(6)
        for d in st:
            pq = _dot(d["t_inv"], jnp.concatenate([_bf(d["at2"]), _bf(d["z2"])], axis=1))
            d["rhs"] = jnp.concatenate(
                [_bf(pq), jnp.concatenate([jnp.zeros_like(d["v2"]), d["v2"]], axis=1)], axis=0)
        boundary(7)
        for d in st:
            d["n1"] = _dot(d["rbk"], d["rhs"])
            d["n2"] = _dot_tn(d["bk2"], d["rhs"])
        boundary(8)
        for j, d in enumerate(st):
            n1, n2 = d["n1"], d["n2"]
            rp2 = d["rt2"] + n1[:, :two]
            m_blk = n2[:, :two] + eye * d["d_end"]
            base = pl.multiple_of(((i + 1) * group + j) * (2 * two), 2 * two)
            rm_scr[pl.ds(base, 2 * two), :] = _bf(jnp.concatenate([rp2, m_blk], axis=0))
            yg_scr[pl.ds(base, 2 * two), :] = jnp.concatenate([n1[:, two:], n2[:, two:]], axis=0)
        h_scr[...] = state[0]
        return 0

    lax.fori_loop(0, n_chunks // group, local, 0)

    def drain(j, _):
        h_scr[...] = serial_step(n_chunks + j, h_scr[...])
        return 0

    lax.fori_loop(0, group, drain, 0)

    ln_w, ln_b = lnw_ref[...], lnb_ref[...]
    inv_n = 1.0 / RWKV_HEAD

    n_fin = 2 if (n_chunks // group) % 2 == 0 else 1

    def finish(i, _):
        blocks = [i * n_fin + t for t in range(n_fin)]
        ys = [y_scr[pl.ds(pl.multiple_of((b + 1) * g_rows, g_rows), g_rows), :] for b in blocks]
        ds = [y - _split_dot(y, head_ones, 2) * inv_n for y in ys]
        vs = [_split_dot(d * d, head_ones, 2) * inv_n for d in ds]
        for b, d, var in zip(blocks, ds, vs):
            rows = pl.ds(pl.multiple_of(b * g_rows, g_rows), g_rows)
            yn = d * lax.rsqrt(var + GN_EPS) * ln_w + ln_b
            o_ref[rows, :] = ((yn + bonus_scr[rows, :]) * g_ref[rows, :]).astype(o_ref.dtype)
        return 0

    lax.fori_loop(0, n_chunks // (group * n_fin), finish, 0)


def _wkv(rkv, cum, a, g, mu_rkv, k_k, k_a, r_k, ln_w, ln_b, *, batch, seq, group=8):
    m, c = cum.shape
    n_pairs = c // PAIR
    n_chunks = seq // CHUNK
    assert seq % (group * CHUNK) == 0
    col = lambda off: pl.BlockSpec((seq, PAIR), lambda b, p: (b, p + off))
    vec = lambda off: pl.BlockSpec((1, PAIR), lambda b, p: (0, p + off))
    vecs = [x.reshape(1, c) for x in (k_k, k_a, r_k, ln_w, ln_b)]
    blk = 6 * _nbytes((seq, PAIR), F32) + _nbytes((seq, PAIR), BF16)
    slot_rows = (n_chunks + group) * 4 * CHUNK
    y_rows = seq + group * CHUNK
    scratch = [pltpu.VMEM((slot_rows, PAIR), BF16),
               pltpu.VMEM((slot_rows, PAIR), F32),
               pltpu.VMEM((seq, PAIR), F32),
               pltpu.VMEM((y_rows, PAIR), F32),
               pltpu.VMEM((PAIR, PAIR), BF16)]
    scr = (_nbytes((slot_rows, PAIR), BF16) + _nbytes((slot_rows, PAIR), F32)
           + _nbytes((seq, PAIR), F32) + _nbytes((y_rows, PAIR), F32))
    return pl.pallas_call(
        functools.partial(_wkv_kernel, n_chunks=n_chunks, group=group),
        grid=(batch, n_pairs),
        in_specs=[col(0), col(n_pairs), col(2 * n_pairs), col(0), col(0), col(0),
                  vec(0), vec(n_pairs), vec(2 * n_pairs)] + [vec(0)] * 5,
        out_specs=col(0),
        out_shape=jax.ShapeDtypeStruct((m, c), BF16),
        scratch_shapes=scratch,
        compiler_params=pltpu.CompilerParams(
            dimension_semantics=("parallel", "parallel"),
            vmem_limit_bytes=int(min(VMEM_CAP, 2 * blk + scr + 8 * 1024 * 1024))),
        name="wkv7_chunked",
    )(rkv, rkv, rkv, cum, a, g, mu_rkv, mu_rkv, mu_rkv, *vecs)


def _rope_tables(seq):
    pos = jnp.arange(seq, dtype=F32)
    inv_freq = ROPE_THETA ** (-jnp.arange(0, ATT_HEAD, 2, dtype=F32) / ATT_HEAD)
    ang = pos[:, None] * inv_freq[None, :]
    cos, sin = jnp.cos(ang), jnp.sin(ang)
    return jnp.concatenate([cos, cos], axis=-1), jnp.concatenate([-sin, sin], axis=-1)


def _pad_to(x, n, axis):
    pad = n - x.shape[axis]
    if pad == 0:
        return x
    widths = [(0, 0)] * x.ndim
    widths[axis] = (0, pad)
    return jnp.pad(x, widths)


def _round_up(n, mult):
    return -(-n // mult) * mult


def _layer(x2, batch, seq, norm_mix_pre, norm_mix_post, norm_ffn_pre, norm_ffn_post, w_in, b_qkv,
           att_sinks, mu_shift, w0, w2, a0, a2, g2, k_k, k_a, r_k, ln_x_w, ln_x_b,
           w_att_branch, w_rwkv_branch, w_out, w_ffn_gate, w_ffn_up, w_ffn_down):
    d = x2.shape[1]
    n_q = att_sinks.shape[0]
    q_w = n_q * ATT_HEAD
    qkv_w = b_qkv.shape[0]
    n_kv = (qkv_w - q_w) // (2 * ATT_HEAD)
    c = w0.shape[0]
    d_decay, d_aaa, d_gate = w2.shape[0], a2.shape[0], g2.shape[0]
    lora_w = d_decay + d_aaa + d_gate
    lora_pad = _round_up(lora_w, LANES)
    ffn = w_ffn_gate.shape[1]

    c0 = qkv_w
    w_att = _bf(w_in[:, :c0])
    w_rkv = _bf(w_in[:, c0:c0 + 3 * c])
    w_lora = _pad_to(_bf(w_in[:, c0 + 3 * c:c0 + 3 * c + lora_w]), lora_pad, 1)
    w_gates = _bf(w_in[:, c0 + 3 * c + lora_w:])
    mu_rkv = mu_shift[:3 * c].reshape(1, 3 * c)
    mu_lora = _pad_to(mu_shift[3 * c:], lora_pad, 0)
    g2_pad = _pad_to(_bf(g2), lora_pad - d_decay - d_aaa, 0)

    (h,) = _norm_call(x2, norm_mix_pre, emit_main=False)

    qkv = _matmul(h, w_att, out_dtype=BF16, bias=b_qkv)
    rkv = _matmul(h, w_rkv, out_dtype=F32)
    lora_raw = _matmul(h, w_lora, out_dtype=F32, tn=lora_pad)
    gates = _matmul(h, w_gates, out_dtype=BF16, act="sigmoid")

    cos, sin = _rope_tables(seq)
    o_att = _attention(qkv, att_sinks.astype(F32), cos, sin, seq=seq, n_q=n_q, n_kv=n_kv)

    cum, a_gate, g_out = _lora(lora_raw, mu_lora, w0, a0, _bf(w2), _bf(a2), g2_pad,
                              seq=seq, d_decay=d_decay, d_aaa=d_aaa)
    o_rwkv = _wkv(rkv, cum, a_gate, g_out, mu_rkv, k_k, k_a, r_k.reshape(-1), ln_x_w, ln_x_b,
                  batch=batch, seq=seq)

    m_att = _matmul(o_att, _bf(w_att_branch), out_dtype=F32, tn=512, mul=gates, mul_col0=0)
    merged = _matmul(o_rwkv, _bf(w_rwkv_branch), out_dtype=BF16, tn=512, mul=gates, mul_col0=d,
                     add=m_att)
    mixed = _matmul(merged, _bf(w_out), out_dtype=F32)
    x1, h2 = _norm_call(mixed, norm_mix_post, res=x2, g_next=norm_ffn_pre)

    tn_ffn = 512
    ffn_pad = _round_up(ffn, 2 * tn_ffn)
    w_gate = _pad_to(_bf(w_ffn_gate), ffn_pad, 1)
    w_up = _pad_to(_bf(w_ffn_up), ffn_pad, 1)
    w_dn = _pad_to(_bf(w_ffn_down), ffn_pad, 0)
    act = _matmul(h2, w_gate, w_up=w_up, out_dtype=BF16, act="swiglu", tn=tn_ffn)
    f = _matmul(act, w_dn, out_dtype=F32, tk=ffn_pad // 4)
    (out,) = _norm_call(f, norm_ffn_post, res=x1)
    return out


def kernel(x, norm_mix_pre, norm_mix_post, norm_ffn_pre, norm_ffn_post, w_in, b_qkv, att_sinks,
           mu_shift, w0, w2, a0, a2, g2, k_k, k_a, r_k, ln_x_w, ln_x_b,
           w_att_branch, w_rwkv_branch, w_out, w_ffn_gate, w_ffn_up, w_ffn_down):
    batch, seq, d = x.shape
    x2 = x.reshape(batch * seq, d)
    for l in range(w_in.shape[0]):
        x2 = _layer(x2, batch, seq, norm_mix_pre[l], norm_mix_post[l], norm_ffn_pre[l],
                    norm_ffn_post[l], w_in[l], b_qkv[l], att_sinks[l], mu_shift[l], w0[l], w2[l],
                    a0[l], a2[l], g2[l], k_k[l], k_a[l], r_k[l], ln_x_w[l], ln_x_b[l],
                    w_att_branch[l], w_rwkv_branch[l], w_out[l],
                    w_ffn_gate[l], w_ffn_up[l], w_ffn_down[l])
    return x2.reshape(batch, seq, d)
```

```python
import functools
import math

import jax
import jax.numpy as jnp
from jax import lax
from jax.experimental import pallas as pl
from jax.experimental.pallas import tpu as pltpu

F32 = jnp.float32
BF16 = jnp.bfloat16

LANES = 128
SUBLANES = 8
V7X_VMEM_BYTES = 64 * 1024 * 1024
VMEM_CAP = V7X_VMEM_BYTES - 6 * 1024 * 1024

ATT_HEAD = 128
ATT_GROUP = 4
WINDOW = 128
ROPE_THETA = 10000.0
NEG_INF = -1e30
RWKV_HEAD = 64
CHUNK = 64
PAIR = 2 * RWKV_HEAD
GN_EPS = 64e-5
RMS_EPS = 1e-6
DECAY_SCALE = math.exp(-0.5)


def _vmem_limit(block_bytes):
    return int(min(VMEM_CAP, 2 * block_bytes + 16 * 1024 * 1024))


def _nbytes(shape, dtype):
    n = 1
    for s in shape:
        n *= s
    return n * jnp.dtype(dtype).itemsize


def _rms(x, g):
    return x * lax.rsqrt(jnp.mean(x * x, axis=-1, keepdims=True) + RMS_EPS) * g


def _norm_kernel(*refs, has_res, emit_main, second_norm):
    it = iter(refs)
    x_ref, g_ref = next(it), next(it)
    res_ref = next(it) if has_res else None
    g2_ref = next(it) if second_norm else None
    main_ref = next(it) if emit_main else None
    next_ref = next(it) if (second_norm or not emit_main) else None
    y = _rms(x_ref[...].astype(F32), g_ref[...])
    if has_res:
        y = res_ref[...] + y
    if emit_main:
        main_ref[...] = y
    if second_norm:
        next_ref[...] = _rms(y, g2_ref[...]).astype(next_ref.dtype)
    elif not emit_main:
        next_ref[...] = y.astype(next_ref.dtype)


def _norm_call(x, g, res=None, g_next=None, *, emit_main=True, tm=256):
    m, d = x.shape
    tm = min(tm, m)
    second_norm = g_next is not None
    assert emit_main or not second_norm
    row = pl.BlockSpec((tm, d), lambda i: (i, 0))
    vec = pl.BlockSpec((1, d), lambda i: (0, 0))
    args, specs = [x, g.reshape(1, d)], [row, vec]
    if res is not None:
        args.append(res); specs.append(row)
    if second_norm:
        args.append(g_next.reshape(1, d)); specs.append(vec)
    out_shape, out_specs = [], []
    if emit_main:
        out_shape.append(jax.ShapeDtypeStruct((m, d), F32)); out_specs.append(row)
    if second_norm or not emit_main:
        out_shape.append(jax.ShapeDtypeStruct((m, d), BF16)); out_specs.append(row)
    blk = _nbytes((tm, d), F32) * (len(args) + len(out_shape))
    outs = pl.pallas_call(
        functools.partial(_norm_kernel, has_res=res is not None, emit_main=emit_main,
                          second_norm=second_norm),
        grid=(m // tm,),
        in_specs=specs, out_specs=out_specs, out_shape=out_shape,
        compiler_params=pltpu.CompilerParams(
            dimension_semantics=("parallel",), vmem_limit_bytes=_vmem_limit(blk)),
        name="rmsnorm",
    )(*args)
    return outs


def _mm_kernel(*refs, nk, has_bias, act, has_mul, has_add):
    it = iter(refs)
    a_ref, w_ref = next(it), next(it)
    up_ref = next(it) if act == "swiglu" else None
    bias_ref = next(it) if has_bias else None
    mul_ref = next(it) if has_mul else None
    add_ref = next(it) if has_add else None
    o_ref = next(it)
    acc_ref = next(it) if nk > 1 else None

    def product():
        return jnp.dot(a_ref[...], w_ref[...], preferred_element_type=F32)

    def finish(y):
        if has_bias:
            y = y + bias_ref[...]
        if act == "sigmoid":
            y = jax.nn.sigmoid(y)
        elif act == "swiglu":
            up = jnp.dot(a_ref[...], up_ref[...], preferred_element_type=F32)
            y = y * jax.nn.sigmoid(y) * up
        if has_mul:
            y = y * mul_ref[...].astype(F32)
        if has_add:
            y = y + add_ref[...].astype(F32)
        o_ref[...] = y.astype(o_ref.dtype)

    if nk == 1:
        finish(product())
    else:
        k = pl.program_id(2)

        @pl.when(k == 0)
        def _():
            acc_ref[...] = product()

        @pl.when(jnp.logical_and(k > 0, k < nk - 1))
        def _():
            acc_ref[...] += product()

        @pl.when(k == nk - 1)
        def _():
            finish(acc_ref[...] + product())


def _matmul(a, w, *, out_dtype, tm=1024, tn=1024, tk=None, bias=None, act=None, w_up=None,
            mul=None, mul_col0=0, add=None, w_col0=0, n_out=None):
    m, kdim = a.shape
    n_out = w.shape[1] if n_out is None else n_out
    tm = min(tm, m)
    tk = kdim if tk is None else tk
    nk = kdim // tk
    assert m % tm == 0 and kdim % tk == 0
    assert n_out % tn == 0 and tn % LANES == 0 and w_col0 % tn == 0
    assert (act == "swiglu") == (w_up is not None) and (w_up is None or nk == 1)
    grid = (m // tm, n_out // tn, nk)
    w_off = w_col0 // tn

    args = [a, w]
    w_spec = pl.BlockSpec((tk, tn), lambda i, j, k: (k, j + w_off))
    specs = [pl.BlockSpec((tm, tk), lambda i, j, k: (i, k)), w_spec]
    blk = _nbytes((tm, tk), a.dtype) + _nbytes((tk, tn), w.dtype)
    if w_up is not None:
        args.append(w_up)
        specs.append(w_spec)
        blk += _nbytes((tk, tn), w_up.dtype)
    if bias is not None:
        args.append(bias.reshape(1, n_out).astype(F32))
        specs.append(pl.BlockSpec((1, tn), lambda i, j, k: (0, j)))
    if mul is not None:
        assert mul_col0 % tn == 0
        off = mul_col0 // tn
        args.append(mul)
        specs.append(pl.BlockSpec((tm, tn), lambda i, j, k: (i, j + off)))
        blk += _nbytes((tm, tn), mul.dtype)
    if add is not None:
        args.append(add)
        specs.append(pl.BlockSpec((tm, tn), lambda i, j, k: (i, j)))
        blk += _nbytes((tm, tn), add.dtype)
    blk += _nbytes((tm, tn), out_dtype)
    scratch = [pltpu.VMEM((tm, tn), F32)] if nk > 1 else []
    extra = _nbytes((tm, tn), F32) * (4 if (nk > 1 or w_up is not None) else 3)

    return pl.pallas_call(
        functools.partial(_mm_kernel, nk=nk, has_bias=bias is not None, act=act,
                          has_mul=mul is not None, has_add=add is not None),
        grid=grid,
        in_specs=specs,
        out_specs=pl.BlockSpec((tm, tn), lambda i, j, k: (i, j)),
        out_shape=jax.ShapeDtypeStruct((m, n_out), out_dtype),
        scratch_shapes=scratch,
        compiler_params=pltpu.CompilerParams(
            dimension_semantics=("parallel", "parallel", "arbitrary"),
            vmem_limit_bytes=int(min(VMEM_CAP, 2 * blk + extra + 4 * 1024 * 1024))),
        name="matmul_" + (act or "plain"),
    )(*args)


def _attn_kernel(sinks_ref, q_ref, kc_ref, kp_ref, vc_ref, vp_ref,
                 cosq_ref, sinq_ref, cosc_ref, sinc_ref, cosp_ref, sinp_ref, o_ref, bias_scr,
                 *, nb, n_kv):
    i = pl.program_id(0)
    first = (i % nb) == 0
    blk = WINDOW

    def rope(t, cos, sin):
        t = t.astype(F32)
        return t * cos + pltpu.roll(t, ATT_HEAD // 2, axis=1) * sin

    cos_q, sin_q = cosq_ref[...], sinq_ref[...]
    cos_c, sin_c = cosc_ref[...], sinc_ref[...]
    cos_p, sin_p = cosp_ref[...], sinp_ref[...]

    qi = lax.broadcasted_iota(jnp.int32, (blk, 2 * blk), 0)
    kj = lax.broadcasted_iota(jnp.int32, (blk, 2 * blk), 1)
    lo = jnp.where(first, blk, 0)
    valid = (kj > qi) & (kj <= qi + blk) & (kj >= lo)
    bias_scr[...] = jnp.where(valid, 0.0, NEG_INF)
    ones = jnp.ones((2 * blk, ATT_HEAD), BF16)

    def scores(h):
        cs = slice(h * ATT_HEAD, (h + 1) * ATT_HEAD)
        k_band = jnp.concatenate(
            [rope(kp_ref[:, cs], cos_p, sin_p), rope(kc_ref[:, cs], cos_c, sin_c)],
            axis=0).astype(BF16)
        q_all = jnp.concatenate(
            [rope(q_ref[:, hq * ATT_HEAD:(hq + 1) * ATT_HEAD], cos_q, sin_q).astype(BF16)
             for hq in range(h * ATT_GROUP, (h + 1) * ATT_GROUP)], axis=0)
        return lax.dot_general(q_all, k_band, (((1,), (1,)), ((), ())),
                               preferred_element_type=F32)

    def finalize(h, pv, sink_parts):
        for g in range(ATT_GROUP):
            hq = h * ATT_GROUP + g
            pv_g = pv[g * blk:(g + 1) * blk]
            o_g = pv_g[:, :ATT_HEAD] / (pv_g[:, ATT_HEAD:] + sink_parts[g])
            o_ref[:, hq * ATT_HEAD:(hq + 1) * ATT_HEAD] = o_g.astype(o_ref.dtype)

    s_next, pending = scores(0), None
    for h in range(n_kv):
        s = s_next
        if h + 1 < n_kv:
            s_next = scores(h + 1)
        cs = slice(h * ATT_HEAD, (h + 1) * ATT_HEAD)
        v_ext = jnp.concatenate(
            [jnp.concatenate([vp_ref[:, cs], vc_ref[:, cs]], axis=0), ones], axis=1)
        p_parts, sink_parts = [], []
        for g in range(ATT_GROUP):
            sink = sinks_ref[h * ATT_GROUP + g]
            s_g = s[g * blk:(g + 1) * blk] + bias_scr[...]
            m_g = jnp.maximum(jnp.max(s_g, axis=-1, keepdims=True), sink)
            p_parts.append(jnp.exp(s_g - m_g).astype(BF16))
            sink_parts.append(jnp.exp(sink - m_g))
        pv = jnp.dot(jnp.concatenate(p_parts, axis=0), v_ext,
                     preferred_element_type=F32)
        if pending is not None:
            finalize(*pending)
        pending = (h, pv, sink_parts)
    finalize(*pending)


def _attention(qkv, sinks, cos, sin, *, seq, n_q, n_kv):
    m = qkv.shape[0]
    nb = seq // WINDOW
    q_w, kv_w = n_q * ATT_HEAD, n_kv * ATT_HEAD
    k_blk0, v_blk0 = q_w // kv_w, q_w // kv_w + 1
    prev = lambda i: jnp.maximum(i - 1, 0)
    scale = ATT_HEAD ** -0.5
    tab_c =pl.BlockSpec((WINDOW, ATT_HEAD), lambda i: (i % nb, 0))
    tab_p = pl.BlockSpec((WINDOW, ATT_HEAD), lambda i: (jnp.maximum(i % nb - 1, 0), 0))
    blk = (_nbytes((WINDOW, q_w), BF16) * 2 + 4 * _nbytes((WINDOW, kv_w), BF16)
           + 4 * _nbytes((WINDOW, ATT_HEAD), F32))
    return pl.pallas_call(
        functools.partial(_attn_kernel, nb=nb, n_kv=n_kv),
        grid=(m // WINDOW,),
        in_specs=[
            pl.BlockSpec(memory_space=pltpu.SMEM),
            pl.BlockSpec((WINDOW, q_w), lambda i: (i, 0)),
            pl.BlockSpec((WINDOW, kv_w), lambda i: (i, k_blk0)),
            pl.BlockSpec((WINDOW, kv_w), lambda i: (prev(i), k_blk0)),
            pl.BlockSpec((WINDOW, kv_w), lambda i: (i, v_blk0)),
            pl.BlockSpec((WINDOW, kv_w), lambda i: (prev(i), v_blk0)),
            tab_c, tab_c, tab_c, tab_c, tab_p, tab_p,
        ],
        out_specs=pl.BlockSpec((WINDOW, q_w), lambda i: (i, 0)),
        out_shape=jax.ShapeDtypeStruct((m, q_w), BF16),
        scratch_shapes=[pltpu.VMEM((WINDOW, 2 * WINDOW), F32)],
        compiler_params=pltpu.CompilerParams(
            dimension_semantics=("parallel",), vmem_limit_bytes=_vmem_limit(blk)),
        name="swa_attention",
    )(sinks, qkv, qkv, qkv, qkv, qkv, cos * scale, sin * scale, cos, sin, cos, sin)


def _lora_kernel(x_ref, prev_ref, mu_ref, w0_ref, a0_ref, w2_ref, a2_ref, g2_ref,
                 cum_ref, a_ref, g_ref, *, seq, d_decay, d_aaa):
    i = pl.program_id(0)
    tm = x_ref.shape[0]
    x = x_ref[...]
    last_prev = prev_ref[prev_ref.shape[0] - 1:, :]
    last_prev = jnp.where((i * tm) % seq == 0, 0.0, last_prev)
    row = lax.broadcasted_iota(jnp.int32, x.shape, 0)
    prev = jnp.where(row == 0, last_prev, pltpu.roll(x, 1, axis=0))
    xs = x + (prev - x) * mu_ref[...]
    p_w = jnp.tanh(xs[:, :d_decay]).astype(BF16)
    p_a = xs[:, d_decay:d_decay + d_aaa].astype(BF16)
    p_g = jax.nn.sigmoid(xs[:, d_decay + d_aaa:]).astype(BF16)
    z = w0_ref[...] + jnp.dot(p_w, w2_ref[...], preferred_element_type=F32)
    log_decay = -DECAY_SCALE * jax.nn.sigmoid(z)
    shift_c = CHUNK.bit_length() - 1
    tr = lax.broadcasted_iota(jnp.int32, (tm, tm), 0)
    tc = lax.broadcasted_iota(jnp.int32, (tm, tm), 1)
    tri_ones = (((tr >> shift_c) == (tc >> shift_c)) & (tr >= tc)).astype(F32).astype(BF16)
    cum_ref[...] = _split_dot(log_decay, tri_ones, 2, ones_left=True)
    a_ref[...] = jax.nn.sigmoid(
        a0_ref[...] + jnp.dot(p_a, a2_ref[...], preferred_element_type=F32))
    g_ref[...] = jnp.dot(p_g, g2_ref[...], preferred_element_type=F32)


def _lora(lora_raw, mu, w0, a0, w2, a2, g2, *, seq, d_decay, d_aaa, tm=256):
    m, wl = lora_raw.shape
    c = w2.shape[1]
    tm = min(tm, seq)
    assert seq % tm == 0 and tm % CHUNK == 0
    sub = SUBLANES
    vec = lambda n: pl.BlockSpec((1, n), lambda i: (0, 0))
    full = lambda a: pl.BlockSpec(a.shape, lambda i: (0, 0))
    out = pl.BlockSpec((tm, c), lambda i: (i, 0))
    blk = (_nbytes((tm, wl), F32) + 3 * _nbytes((tm, c), F32)
           + _nbytes(w2.shape, BF16) + _nbytes(a2.shape, BF16) + _nbytes(g2.shape, BF16))
    return pl.pallas_call(
        functools.partial(_lora_kernel, seq=seq, d_decay=d_decay, d_aaa=d_aaa),
        grid=(m // tm,),
        in_specs=[
            pl.BlockSpec((tm, wl), lambda i: (i, 0)),
            pl.BlockSpec((sub, wl), lambda i: (jnp.maximum(i * (tm // sub) - 1, 0), 0)),
            vec(wl), vec(c), vec(c), full(w2), full(a2), full(g2),
        ],
        out_specs=[out, out, out],
        out_shape=[jax.ShapeDtypeStruct((m, c), F32)] * 3,
        compiler_params=pltpu.CompilerParams(
            dimension_semantics=("parallel",), vmem_limit_bytes=_vmem_limit(blk)),
        name="rwkv_lora",
    )(lora_raw, lora_raw, mu.reshape(1, wl), w0.reshape(1, c), a0.reshape(1, c), w2, a2, g2)


def _bf(x):
    return x.astype(BF16)


def _dot(a, b):
    return jnp.dot(_bf(a), _bf(b), preferred_element_type=F32)


def _dot_nt(a, b):
    return lax.dot_general(_bf(a), _bf(b), (((1,), (1,)), ((), ())), preferred_element_type=F32)


def _dot_tn(a, b):
    return lax.dot_general(_bf(a), _bf(b), (((0,), (0,)), ((), ())), preferred_element_type=F32)


def _split_dot(x, ones_bf, parts, ones_left=False):
    acc, rem = None, x
    for p in range(parts):
        hi = _bf(rem)
        ops = (ones_bf, hi) if ones_left else (hi, ones_bf)
        term = jnp.dot(*ops, preferred_element_type=F32)
        acc = term if acc is None else acc + term
        if p + 1 < parts:
            rem = rem - hi.astype(F32)
    return acc


def _wkv_kernel(r_ref, k_ref, v_ref, cum_ref, a_ref, g_ref,
                mur_ref, muk_ref, muv_ref, kk_ref, ka_ref, rk_ref, lnw_ref, lnb_ref,
                o_ref, rm_scr, yg_scr, bonus_scr, y_scr, h_scr, *, n_chunks, group):
    c_len, two = CHUNK, 2 * CHUNK
    g_rows = group * c_len
    shift_c = CHUNK.bit_length() - 1
    lane = lax.broadcasted_iota(jnp.int32, (c_len, PAIR), 1)
    m0 = (lane < RWKV_HEAD).astype(F32)
    m1 = 1.0 - m0
    row = lax.broadcasted_iota(jnp.int32, (two, two), 0)
    col = lax.broadcasted_iota(jnp.int32, (two, two), 1)
    same = (row >> shift_c) == (col >> shift_c)
    strict = same & ((row & (CHUNK - 1)) > (col & (CHUNK - 1)))
    incl = same & ((row & (CHUNK - 1)) >= (col & (CHUNK - 1)))
    eye = (row == col).astype(F32)
    head_ones = _bf(same.astype(F32))
    g_row = lax.broadcasted_iota(jnp.int32, (g_rows, PAIR), 0)
    row0 = g_row == 0
    chunk_row0 = (g_row & (CHUNK - 1)) == 0

    mu_r, mu_k, mu_v = mur_ref[...], muk_ref[...], muv_ref[...]
    k_k, k_a, r_k = kk_ref[...], ka_ref[...], rk_ref[...]

    def stack(x):
        return jnp.concatenate([x * m0, x * m1], axis=0)

    rm_scr[pl.ds(0, group * 2 * two), :] = jnp.zeros((group * 2 * two, PAIR), BF16)
    yg_scr[pl.ds(0, group * 2 * two), :] = jnp.zeros((group * 2 * two, PAIR), F32)
    h_scr[...] = jnp.zeros_like(h_scr)
    y_scr[pl.ds(0, g_rows), :] = jnp.zeros((g_rows, PAIR), F32)

    def serial_step(slot, h):
        base = pl.multiple_of(slot * (2 * two), 2 * two)
        out = jnp.dot(rm_scr[pl.ds(base, 2 * two), :], h,
                      preferred_element_type=F32) + yg_scr[pl.ds(base, 2 * two), :]
        rows = pl.ds(pl.multiple_of((slot + group) * c_len, c_len), c_len)
        y_scr[rows, :] = out[:c_len] + out[c_len:two]
        return _bf(out[two:])

    n_boundaries = 9
    serial_at = [j * n_boundaries // group for j in range(group)]
    n_groups = n_chunks // group
    ln_w, ln_b = lnw_ref[...], lnb_ref[...]
    inv_n = 1.0 / RWKV_HEAD

    def group_rows(gi):
        start = gi * g_rows
        return pl.ds(start if isinstance(gi, int) else pl.multiple_of(start, g_rows), g_rows)

    def finish(y_blocks, groups):
        rows = [group_rows(gi) for gi in groups]
        ys = [y_scr[group_rows(yb), :] for yb in y_blocks]
        ds = [y - _split_dot(y, head_ones, 1) * inv_n for y in ys]
        vs = [_split_dot(d * d, head_ones, 1) * inv_n for d in ds]
        return [((d * lax.rsqrt(var + GN_EPS) * ln_w + ln_b + bonus_scr[r, :])
                 * g_ref[r, :]).astype(o_ref.dtype) for r, d, var in zip(rows, ds, vs)]

    def local(i, _):
        rows = pl.ds(pl.multiple_of(i * g_rows, g_rows), g_rows)
        before = pl.ds(pl.multiple_of(jnp.maximum(i * g_rows - SUBLANES, 0), SUBLANES), SUBLANES)
        keep_prev = jnp.where(i > 0, 1.0, 0.0)
        state = [h_scr[...]]

        def boundary(k):
            for j in range(group):
                if serial_at[j] == k:
                    state[0] = serial_step(i * group + j, state[0])

        def shift(ref, mu):
            x = ref[rows, :]
            last = ref[before, :][SUBLANES - 1:, :] * keep_prev
            prev = jnp.where(row0, last, pltpu.roll(x, 1, axis=0))
            return x + (prev - x) * mu

        rs, ks, vs = shift(r_ref, mu_r), shift(k_ref, mu_k), shift(v_ref, mu_v)
        cum, a = cum_ref[rows, :], a_ref[rows, :]

        kk = ks * k_k
        norm = jnp.sqrt(_split_dot(kk * kk, head_ones, 1))
        kk = kk / jnp.maximum(norm, 1e-12)
        kx = ks * (1.0 + (a - 1.0) * k_a)
        b = kk * a
        bonus_scr[rows, :] = _split_dot(rs * kx * r_k, head_ones, 1) * vs
        (o_block,) = finish([i], [jnp.maximum(i - 2, 0)])

        cum_before = jnp.where(chunk_row0, 0.0, pltpu.roll(cum, 1, axis=0))
        e_neg = jnp.exp(-cum)
        at_g = -kk * jnp.exp(cum_before)
        rt_g = rs * jnp.exp(cum)
        bt_g, kt_g = _bf(b * e_neg), _bf(kx * e_neg)

        st = []
        for j in range(group):
            sl = slice(j * c_len, (j + 1) * c_len)
            c_end = cum[(j + 1) * c_len - 1:(j + 1) * c_len, :]
            e_rem = jnp.exp(c_end - cum[sl])
            at2, rt2 = stack(at_g[sl]), stack(rt_g[sl])
            st.append(dict(
                at2=at2, rt2=rt2, v2=_bf(stack(vs[sl])), d_end=jnp.exp(c_end),
                bk2=jnp.concatenate([stack(b[sl] * e_rem), stack(kx[sl] * e_rem)], axis=0),
                s=_dot_nt(jnp.concatenate([at2, rt2], axis=0),
                          jnp.concatenate([bt_g[sl], bt_g[sl], kt_g[sl], kt_g[sl]], axis=0))))
        boundary(0)
        for d in st:
            s = d.pop("s")
            x = jnp.where(strict, s[:two, :two], 0.0)
            a_ak = jnp.where(strict, s[:two, two:], 0.0)
            d["rbk"] = jnp.concatenate([jnp.where(incl, s[two:, :two], 0.0),
                                        jnp.where(incl, s[two:, two:], 0.0)], axis=1)
            d["t_sum"] = eye + x
            d["x_pow"] = _dot(x, x)
            d["z2"] = _dot(a_ak, d["v2"])
        boundary(1)
        for step in range(CHUNK.bit_length() - 3):
            for d in st:
                both = _dot(d["x_pow"], jnp.concatenate([d["t_sum"], d["x_pow"]], axis=1))
                d["t_sum"] = d["t_sum"] + both[:, :two]
                d["x_pow"] = both[:, two:]
            boundary(2 + step)
        for d in st:
            d["t_inv"] = d["t_sum"] + _dot(d["x_pow"], d["t_sum"])
        boundary(6)
        for d in st:
            pq = _dot(d["t_inv"], jnp.concatenate([_bf(d["at2"]), _bf(d["z2"])], axis=1))
            d["rhs"] = jnp.concatenate(
                [_bf(pq), jnp.concatenate([jnp.zeros_like(d["v2"]), d["v2"]], axis=1)], axis=0)
        boundary(7)
        for d in st:
            d["n1"] = _dot(d["rbk"], d["rhs"])
            d["n2"] = _dot_tn(d["bk2"], d["rhs"])
        boundary(8)
        for j, d in enumerate(st):
            n1, n2 = d["n1"], d["n2"]
            rp2 = d["rt2"] + n1[:, :two]
            m_blk = n2[:, :two] + eye * d["d_end"]
            base = pl.multiple_of(((i + 1) * group + j) * (2 * two), 2 * two)
            rm_scr[pl.ds(base, 2 * two), :] = _bf(jnp.concatenate([rp2, m_blk], axis=0))
            yg_scr[pl.ds(base, 2 * two), :] = jnp.concatenate([n1[:, two:], n2[:, two:]], axis=0)
        h_scr[...] = state[0]

        @pl.when(i >= 2)
        def _():
            o_ref[group_rows(jnp.maximum(i - 2, 0)), :] = o_block

        return 0

    lax.fori_loop(0, n_groups, local, 0)

    def drain(j, _):
        h_scr[...] = serial_step(n_chunks + j, h_scr[...])
        return 0

    lax.fori_loop(0, group, drain, 0)

    tail = list(range(max(n_groups - 2, 0), n_groups))
    for gi, o_block in zip(tail, finish([gi + 2 for gi in tail], tail)):
        o_ref[group_rows(gi), :] = o_block


def _wkv(rkv, cum, a, g, mu_rkv, k_k, k_a, r_k, ln_w, ln_b, *, batch, seq, group=8):
    m, c = cum.shape
    n_pairs = c // PAIR
    n_chunks = seq // CHUNK
    assert seq % (group * CHUNK) == 0
    col = lambda off: pl.BlockSpec((seq, PAIR), lambda b, p: (b, p + off))
    vec = lambda off: pl.BlockSpec((1, PAIR), lambda b, p: (0, p + off))
    vecs = [x.reshape(1, c) for x in (k_k, k_a, r_k, ln_w, ln_b)]
    blk = 6 * _nbytes((seq, PAIR), F32) + _nbytes((seq, PAIR), BF16)
    slot_rows = (n_chunks + group) * 4 * CHUNK
    y_rows = seq + 2 * group * CHUNK
    scratch = [pltpu.VMEM((slot_rows, PAIR), BF16),
               pltpu.VMEM((slot_rows, PAIR), F32),
               pltpu.VMEM((seq, PAIR), F32),
               pltpu.VMEM((y_rows, PAIR), F32),
               pltpu.VMEM((PAIR, PAIR), BF16)]
    scr = (_nbytes((slot_rows, PAIR), BF16) + _nbytes((slot_rows, PAIR), F32)
           + _nbytes((seq, PAIR), F32) + _nbytes((y_rows, PAIR), F32))
    return pl.pallas_call(
        functools.partial(_wkv_kernel, n_chunks=n_chunks, group=group),
        grid=(batch, n_pairs),
        in_specs=[col(0), col(n_pairs), col(2 * n_pairs), col(0), col(0), col(0),
                  vec(0), vec(n_pairs), vec(2 * n_pairs)] + [vec(0)] * 5,
        out_specs=col(0),
        out_shape=jax.ShapeDtypeStruct((m, c), BF16),
        scratch_shapes=scratch,
        compiler_params=pltpu.CompilerParams(
            dimension_semantics=("parallel", "parallel"),
            vmem_limit_bytes=int(min(VMEM_CAP, 2 * blk + scr + 8 * 1024 * 1024))),
        name="wkv7_chunked",
    )(rkv, rkv, rkv, cum, a, g, mu_rkv, mu_rkv, mu_rkv, *vecs)


def _rope_tables(seq):
    pos = jnp.arange(seq, dtype=F32)
    inv_freq = ROPE_THETA ** (-jnp.arange(0, ATT_HEAD, 2, dtype=F32) / ATT_HEAD)
    ang = pos[:, None] * inv_freq[None, :]
    cos, sin = jnp.cos(ang), jnp.sin(ang)
    return jnp.concatenate([cos, cos], axis=-1), jnp.concatenate([-sin, sin], axis=-1)


def _pad_to(x, n, axis):
    pad = n - x.shape[axis]
    if pad == 0:
        return x
    widths = [(0, 0)] * x.ndim
    widths[axis] = (0, pad)
    return jnp.pad(x, widths)


def _round_up(n, mult):
    return -(-n // mult) * mult


def _layer(x2, batch, seq, norm_mix_pre, norm_mix_post, norm_ffn_pre, norm_ffn_post, w_in, b_qkv,
           att_sinks, mu_shift, w0, w2, a0, a2, g2, k_k, k_a, r_k, ln_x_w, ln_x_b,
           w_att_branch, w_rwkv_branch, w_out, w_ffn_gate, w_ffn_up, w_ffn_down):
    d = x2.shape[1]
    n_q = att_sinks.shape[0]
    q_w = n_q * ATT_HEAD
    qkv_w = b_qkv.shape[0]
    n_kv = (qkv_w - q_w) // (2 * ATT_HEAD)
    c = w0.shape[0]
    d_decay, d_aaa, d_gate = w2.shape[0], a2.shape[0], g2.shape[0]
    lora_w = d_decay + d_aaa + d_gate
    lora_pad = _round_up(lora_w, LANES)
    ffn = w_ffn_gate.shape[1]

    c0 = qkv_w
    lora_col0 = c0 + 3 * c
    w_in_bf = _bf(w_in)
    w_gates = w_in_bf[:, lora_col0 + lora_w:]
    mu_rkv = mu_shift[:3 * c].reshape(1, 3 * c)
    mu_lora = _pad_to(mu_shift[3 * c:], lora_pad, 0)
    g2_pad = _pad_to(_bf(g2), lora_pad - d_decay - d_aaa, 0)

    (h,) = _norm_call(x2, norm_mix_pre, emit_main=False)

    qkv = _matmul(h, w_in_bf, n_out=c0, out_dtype=BF16, bias=b_qkv)
    rkv = _matmul(h, w_in_bf, w_col0=c0, n_out=3 * c, out_dtype=F32)
    lora_raw = _matmul(h, w_in_bf, w_col0=lora_col0, n_out=lora_pad, out_dtype=F32, tn=lora_pad)
    gates = _matmul(h, w_gates, out_dtype=BF16, act="sigmoid")

    cos, sin = _rope_tables(seq)
    o_att = _attention(qkv, att_sinks.astype(F32), cos, sin, seq=seq, n_q=n_q, n_kv=n_kv)

    cum, a_gate, g_out = _lora(lora_raw, mu_lora, w0, a0, _bf(w2), _bf(a2), g2_pad,
                              seq=seq, d_decay=d_decay, d_aaa=d_aaa)
    o_rwkv = _wkv(rkv, cum, a_gate, g_out, mu_rkv, k_k, k_a, r_k.reshape(-1), ln_x_w, ln_x_b,
                  batch=batch, seq=seq)

    m_att = _matmul(o_att, _bf(w_att_branch), out_dtype=F32, tn=512, mul=gates, mul_col0=0)
    merged = _matmul(o_rwkv, _bf(w_rwkv_branch), out_dtype=BF16, tn=512, mul=gates, mul_col0=d,
                     add=m_att)
    mixed = _matmul(merged, _bf(w_out), out_dtype=BF16)
    x1, h2 = _norm_call(mixed, norm_mix_post, res=x2, g_next=norm_ffn_pre)

    tn_ffn = 512
    ffn_pad = _round_up(ffn, 2 * tn_ffn)
    w_gate = _pad_to(_bf(w_ffn_gate), ffn_pad, 1)
    w_up = _pad_to(_bf(w_ffn_up), ffn_pad, 1)
    w_dn = _pad_to(_bf(w_ffn_down), ffn_pad, 0)
    act = _matmul(h2, w_gate, w_up=w_up, out_dtype=BF16, act="swiglu", tn=tn_ffn)
    f = _matmul(act, w_dn, out_dtype=BF16, tk=ffn_pad // 4)
    (out,) = _norm_call(f, norm_ffn_post, res=x1)
    return out


def kernel(x, norm_mix_pre, norm_mix_post, norm_ffn_pre, norm_ffn_post, w_in, b_qkv, att_sinks,
           mu_shift, w0, w2, a0, a2, g2, k_k, k_a, r_k, ln_x_w, ln_x_b,
           w_att_branch, w_rwkv_branch, w_out, w_ffn_gate, w_ffn_up, w_ffn_down):
    batch, seq, d = x.shape
    x2 = x.reshape(batch * seq, d)
    for l in range(w_in.shape[0]):
        x2 = _layer(x2, batch, seq, norm_mix_pre[l], norm_mix_post[l], norm_ffn_pre[l],
                    norm_ffn_post[l], w_in[l], b_qkv[l], att_sinks[l], mu_shift[l], w0[l], w2[l],
                    a0[l], a2[l], g2[l], k_k[l], k_a[l], r_k[l], ln_x_w[l], ln_x_b[l],
                    w_att_branch[l], w_rwkv_branch[l], w_out[l],
                    w_ffn_gate[l], w_ffn_up[l], w_ffn_down[l])
    return x2.reshape(batch, seq, d)
```

```python
import functools
import math

import jax
import jax.numpy as jnp
from jax import lax
from jax.experimental import pallas as pl
from jax.experimental.pallas import tpu as pltpu

F32 = jnp.float32
BF16 = jnp.bfloat16

LANES = 128
SUBLANES = 8
V7X_VMEM_BYTES = 64 * 1024 * 1024
VMEM_CAP = V7X_VMEM_BYTES - 6 * 1024 * 1024

ATT_HEAD = 128
ATT_GROUP = 4
WINDOW = 128
ROPE_THETA = 10000.0
NEG_INF = -1e30
RWKV_HEAD = 64
CHUNK = 64
PAIR = 2 * RWKV_HEAD
GN_EPS = 64e-5
RMS_EPS = 1e-6
DECAY_SCALE = math.exp(-0.5)


def _vmem_limit(block_bytes):
    return int(min(VMEM_CAP, 2 * block_bytes + 16 * 1024 * 1024))


def _nbytes(shape, dtype):
    n = 1
    for s in shape:
        n *= s
    return n * jnp.dtype(dtype).itemsize


def _cast_kernel(x_ref, *o_refs, windows, n_row_blocks, pad_rows):
    i = pl.program_id(0)
    x = x_ref[...]
    for o_ref, (c0, width) in zip(o_refs, windows):
        o_ref[:, :width] = x[:, c0:c0 + width].astype(o_ref.dtype)
        if o_ref.shape[1] > width:
            o_ref[:, width:] = jnp.zeros((o_ref.shape[0], o_ref.shape[1] - width), o_ref.dtype)
        if pad_rows:
            @pl.when(i >= n_row_blocks)
            def _():
                o_ref[...] = jnp.zeros_like(o_ref)


def _cast_bf16(w, windows, *, rows_out=None, tr=256):
    rows, cols = w.shape
    rows_out = rows if rows_out is None else rows_out
    tr = min(tr, rows)
    assert rows % tr == 0 and rows_out % tr == 0
    n_in = rows // tr
    blk = _nbytes((tr, cols), F32) + sum(_nbytes((tr, wp), BF16) for _, _, wp in windows)
    return pl.pallas_call(
        functools.partial(_cast_kernel, windows=[(c0, wd) for c0, wd, _ in windows],
                          n_row_blocks=n_in, pad_rows=rows_out > rows),
        grid=(rows_out // tr,),
        in_specs=[pl.BlockSpec((tr, cols), lambda i: (jnp.minimum(i, n_in - 1), 0))],
        out_specs=[pl.BlockSpec((tr, wp), lambda i: (i, 0)) for _, _, wp in windows],
        out_shape=[jax.ShapeDtypeStruct((rows_out, wp), BF16) for _, _, wp in windows],
        compiler_params=pltpu.CompilerParams(
            dimension_semantics=("parallel",), vmem_limit_bytes=_vmem_limit(blk)),
        name="cast_bf16",
    )(w)


def _rms(x, g):
    return x * lax.rsqrt(jnp.mean(x * x, axis=-1, keepdims=True) + RMS_EPS) * g


def _norm_kernel(*refs, has_res, emit_main, second_norm):
    it = iter(refs)
    x_ref, g_ref = next(it), next(it)
    res_ref = next(it) if has_res else None
    g2_ref = next(it) if second_norm else None
    main_ref = next(it) if emit_main else None
    next_ref = next(it) if (second_norm or not emit_main) else None
    y = _rms(x_ref[...].astype(F32), g_ref[...])
    if has_res:
        y = res_ref[...] + y
    if emit_main:
        main_ref[...] = y
    if second_norm:
        next_ref[...] = _rms(y, g2_ref[...]).astype(next_ref.dtype)
    elif not emit_main:
        next_ref[...] = y.astype(next_ref.dtype)


def _norm_call(x, g, res=None, g_next=None, *, emit_main=True, tm=256):
    m, d = x.shape
    tm = min(tm, m)
    second_norm = g_next is not None
    assert emit_main or not second_norm
    row = pl.BlockSpec((tm, d), lambda i: (i, 0))
    vec = pl.BlockSpec((1, d), lambda i: (0, 0))
    args, specs = [x, g.reshape(1, d)], [row, vec]
    if res is not None:
        args.append(res); specs.append(row)
    if second_norm:
        args.append(g_next.reshape(1, d)); specs.append(vec)
    out_shape, out_specs = [], []
    if emit_main:
        out_shape.append(jax.ShapeDtypeStruct((m, d), F32)); out_specs.append(row)
    if second_norm or not emit_main:
        out_shape.append(jax.ShapeDtypeStruct((m, d), BF16)); out_specs.append(row)
    blk = _nbytes((tm, d), F32) * (len(args) + len(out_shape))
    outs = pl.pallas_call(
        functools.partial(_norm_kernel, has_res=res is not None, emit_main=emit_main,
                          second_norm=second_norm),
        grid=(m // tm,),
        in_specs=specs, out_specs=out_specs, out_shape=out_shape,
        compiler_params=pltpu.CompilerParams(
            dimension_semantics=("parallel",), vmem_limit_bytes=_vmem_limit(blk)),
        name="rmsnorm",
    )(*args)
    return outs


def _mm_kernel(*refs, nk, has_bias, act, has_mul, has_add):
    it = iter(refs)
    a_ref, w_ref = next(it), next(it)
    up_ref = next(it) if act == "swiglu" else None
    bias_ref = next(it) if has_bias else None
    mul_ref = next(it) if has_mul else None
    add_ref = next(it) if has_add else None
    o_ref = next(it)
    acc_ref = next(it) if nk > 1 else None

    def product():
        return jnp.dot(a_ref[...], w_ref[...], preferred_element_type=F32)

    def finish(y):
        if has_bias:
            y = y + bias_ref[...]
        if act == "sigmoid":
            y = jax.nn.sigmoid(y)
        elif act == "swiglu":
            up = jnp.dot(a_ref[...], up_ref[...], preferred_element_type=F32)
            y = y * jax.nn.sigmoid(y) * up
        if has_mul:
            y = y * mul_ref[...].astype(F32)
        if has_add:
            y = y + add_ref[...].astype(F32)
        o_ref[...] = y.astype(o_ref.dtype)

    if nk == 1:
        finish(product())
    else:
        k = pl.program_id(2)

        @pl.when(k == 0)
        def _():
            acc_ref[...] = product()

        @pl.when(jnp.logical_and(k > 0, k < nk - 1))
        def _():
            acc_ref[...] += product()

        @pl.when(k == nk - 1)
        def _():
            finish(acc_ref[...] + product())


def _matmul(a, w, *, out_dtype, tm=1024, tn=1024, tk=None, bias=None, act=None, w_up=None,
            mul=None, mul_col0=0, add=None, w_col0=0, n_out=None):
    m, kdim = a.shape
    n_out = w.shape[1] if n_out is None else n_out
    tm = min(tm, m)
    tk = kdim if tk is None else tk
    nk = kdim // tk
    assert m % tm == 0 and kdim % tk == 0
    assert n_out % tn == 0 and tn % LANES == 0 and w_col0 % tn == 0
    assert (act == "swiglu") == (w_up is not None) and (w_up is None or nk == 1)
    grid = (m // tm, n_out // tn, nk)
    w_off = w_col0 // tn

    args = [a, w]
    w_spec = pl.BlockSpec((tk, tn), lambda i, j, k: (k, j + w_off))
    specs = [pl.BlockSpec((tm, tk), lambda i, j, k: (i, k)), w_spec]
    blk = _nbytes((tm, tk), a.dtype) + _nbytes((tk, tn), w.dtype)
    if w_up is not None:
        args.append(w_up)
        specs.append(w_spec)
        blk += _nbytes((tk, tn), w_up.dtype)
    if bias is not None:
        args.append(bias.reshape(1, n_out).astype(F32))
        specs.append(pl.BlockSpec((1, tn), lambda i, j, k: (0, j)))
    if mul is not None:
        assert mul_col0 % tn == 0
        off = mul_col0 // tn
        args.append(mul)
        specs.append(pl.BlockSpec((tm, tn), lambda i, j, k: (i, j + off)))
        blk += _nbytes((tm, tn), mul.dtype)
    if add is not None:
        args.append(add)
        specs.append(pl.BlockSpec((tm, tn), lambda i, j, k: (i, j)))
        blk += _nbytes((tm, tn), add.dtype)
    blk += _nbytes((tm, tn), out_dtype)
    scratch = [pltpu.VMEM((tm, tn), F32)] if nk > 1 else []
    extra = _nbytes((tm, tn), F32) * (4 if (nk > 1 or w_up is not None) else 3)

    return pl.pallas_call(
        functools.partial(_mm_kernel, nk=nk, has_bias=bias is not None, act=act,
                          has_mul=mul is not None, has_add=add is not None),
        grid=grid,
        in_specs=specs,
        out_specs=pl.BlockSpec((tm, tn), lambda i, j, k: (i, j)),
        out_shape=jax.ShapeDtypeStruct((m, n_out), out_dtype),
        scratch_shapes=scratch,
        compiler_params=pltpu.CompilerParams(
            dimension_semantics=("parallel", "parallel", "arbitrary"),
            vmem_limit_bytes=int(min(VMEM_CAP, 2 * blk + extra + 4 * 1024 * 1024))),
        name="matmul_" + (act or "plain"),
    )(*args)


def _attn_kernel(sinks_ref, q_ref, kc_ref, kp_ref, vc_ref, vp_ref,
                 cosq_ref, sinq_ref, cosc_ref, sinc_ref, cosp_ref, sinp_ref, o_ref, bias_scr,
                 *, nb, n_kv):
    i = pl.program_id(0)
    first = (i % nb) == 0
    blk = WINDOW

    def rope(t, cos, sin):
        t = t.astype(F32)
        return t * cos + pltpu.roll(t, ATT_HEAD // 2, axis=1) * sin

    cos_q, sin_q = cosq_ref[...], sinq_ref[...]
    cos_c, sin_c = cosc_ref[...], sinc_ref[...]
    cos_p, sin_p = cosp_ref[...], sinp_ref[...]

    qi = lax.broadcasted_iota(jnp.int32, (blk, 2 * blk), 0)
    kj = lax.broadcasted_iota(jnp.int32, (blk, 2 * blk), 1)
    lo = jnp.where(first, blk, 0)
    valid = (kj > qi) & (kj <= qi + blk) & (kj >= lo)
    bias_scr[...] = jnp.where(valid, 0.0, NEG_INF)
    ones = jnp.ones((2 * blk, ATT_HEAD), BF16)

    def scores(h):
        cs = slice(h * ATT_HEAD, (h + 1) * ATT_HEAD)
        k_band = jnp.concatenate(
            [rope(kp_ref[:, cs], cos_p, sin_p), rope(kc_ref[:, cs], cos_c, sin_c)],
            axis=0).astype(BF16)
        q_all = jnp.concatenate(
            [rope(q_ref[:, hq * ATT_HEAD:(hq + 1) * ATT_HEAD], cos_q, sin_q).astype(BF16)
             for hq in range(h * ATT_GROUP, (h + 1) * ATT_GROUP)], axis=0)
        return lax.dot_general(q_all, k_band, (((1,), (1,)), ((), ())),
                               preferred_element_type=F32)

    def finalize(h, pv, sink_parts):
        for g in range(ATT_GROUP):
            hq = h * ATT_GROUP + g
            pv_g = pv[g * blk:(g + 1) * blk]
            o_g = pv_g[:, :ATT_HEAD] / (pv_g[:, ATT_HEAD:] + sink_parts[g])
            o_ref[:, hq * ATT_HEAD:(hq + 1) * ATT_HEAD] = o_g.astype(o_ref.dtype)

    s_next, pending = scores(0), None
    for h in range(n_kv):
        s = s_next
        if h + 1 < n_kv:
            s_next = scores(h + 1)
        cs = slice(h * ATT_HEAD, (h + 1) * ATT_HEAD)
        v_ext = jnp.concatenate(
            [jnp.concatenate([vp_ref[:, cs], vc_ref[:, cs]], axis=0), ones], axis=1)
        p_parts, sink_parts = [], []
        for g in range(ATT_GROUP):
            sink = sinks_ref[h * ATT_GROUP + g]
            s_g = s[g * blk:(g + 1) * blk] + bias_scr[...]
            m_g = jnp.maximum(jnp.max(s_g, axis=-1, keepdims=True), sink)
            p_parts.append(jnp.exp(s_g - m_g).astype(BF16))
            sink_parts.append(jnp.exp(sink - m_g))
        pv = jnp.dot(jnp.concatenate(p_parts, axis=0), v_ext,
                     preferred_element_type=F32)
        if pending is not None:
            finalize(*pending)
        pending = (h, pv, sink_parts)
    finalize(*pending)


def _attention(qkv, sinks, cos, sin, *, seq, n_q, n_kv):
    m = qkv.shape[0]
    nb = seq // WINDOW
    q_w, kv_w = n_q * ATT_HEAD, n_kv * ATT_HEAD
    k_blk0, v_blk0 = q_w // kv_w, q_w // kv_w + 1
    prev = lambda i: jnp.maximum(i - 1, 0)
    scale = ATT_HEAD ** -0.5
    tab_c =pl.BlockSpec((WINDOW, ATT_HEAD), lambda i: (i % nb, 0))
    tab_p = pl.BlockSpec((WINDOW, ATT_HEAD), lambda i: (jnp.maximum(i % nb - 1, 0), 0))
    blk = (_nbytes((WINDOW, q_w), BF16) * 2 + 4 * _nbytes((WINDOW, kv_w), BF16)
           + 4 * _nbytes((WINDOW, ATT_HEAD), F32))
    return pl.pallas_call(
        functools.partial(_attn_kernel, nb=nb, n_kv=n_kv),
        grid=(m // WINDOW,),
        in_specs=[
            pl.BlockSpec(memory_space=pltpu.SMEM),
            pl.BlockSpec((WINDOW, q_w), lambda i: (i, 0)),
            pl.BlockSpec((WINDOW, kv_w), lambda i: (i, k_blk0)),
            pl.BlockSpec((WINDOW, kv_w), lambda i: (prev(i), k_blk0)),
            pl.BlockSpec((WINDOW, kv_w), lambda i: (i, v_blk0)),
            pl.BlockSpec((WINDOW, kv_w), lambda i: (prev(i), v_blk0)),
            tab_c, tab_c, tab_c, tab_c, tab_p, tab_p,
        ],
        out_specs=pl.BlockSpec((WINDOW, q_w), lambda i: (i, 0)),
        out_shape=jax.ShapeDtypeStruct((m, q_w), BF16),
        scratch_shapes=[pltpu.VMEM((WINDOW, 2 * WINDOW), F32)],
        compiler_params=pltpu.CompilerParams(
            dimension_semantics=("parallel",), vmem_limit_bytes=_vmem_limit(blk)),
        name="swa_attention",
    )(sinks, qkv, qkv, qkv, qkv, qkv, cos * scale, sin * scale, cos, sin, cos, sin)


def _lora_kernel(x_ref, prev_ref, mu_ref, w0_ref, a0_ref, w2_ref, a2_ref, g2_ref,
                 cum_ref, a_ref, g_ref, *, seq, d_decay, d_aaa):
    i = pl.program_id(0)
    tm = x_ref.shape[0]
    x = x_ref[...]
    last_prev = prev_ref[prev_ref.shape[0] - 1:, :]
    last_prev = jnp.where((i * tm) % seq == 0, 0.0, last_prev)
    row = lax.broadcasted_iota(jnp.int32, x.shape, 0)
    prev = jnp.where(row == 0, last_prev, pltpu.roll(x, 1, axis=0))
    xs = x + (prev - x) * mu_ref[...]
    p_w = jnp.tanh(xs[:, :d_decay]).astype(BF16)
    p_a = xs[:, d_decay:d_decay + d_aaa].astype(BF16)
    p_g = jax.nn.sigmoid(xs[:, d_decay + d_aaa:]).astype(BF16)
    z = w0_ref[...] + jnp.dot(p_w, w2_ref[...], preferred_element_type=F32)
    log_decay = -DECAY_SCALE * jax.nn.sigmoid(z)
    shift_c = CHUNK.bit_length() - 1
    tr = lax.broadcasted_iota(jnp.int32, (tm, tm), 0)
    tc = lax.broadcasted_iota(jnp.int32, (tm, tm), 1)
    tri_ones = (((tr >> shift_c) == (tc >> shift_c)) & (tr >= tc)).astype(F32).astype(BF16)
    cum_ref[...] = _split_dot(log_decay, tri_ones, 2, ones_left=True)
    a_ref[...] = jax.nn.sigmoid(
        a0_ref[...] + jnp.dot(p_a, a2_ref[...], preferred_element_type=F32)).astype(a_ref.dtype)
    g_ref[...] = jnp.dot(p_g, g2_ref[...], preferred_element_type=F32).astype(g_ref.dtype)


def _lora(lora_raw, mu, w0, a0, w2, a2, g2, *, seq, d_decay, d_aaa, tm=256):
    m, wl = lora_raw.shape
    c = w2.shape[1]
    tm = min(tm, seq)
    assert seq % tm == 0 and tm % CHUNK == 0
    sub = SUBLANES
    vec = lambda n: pl.BlockSpec((1, n), lambda i: (0, 0))
    full = lambda a: pl.BlockSpec(a.shape, lambda i: (0, 0))
    out = pl.BlockSpec((tm, c), lambda i: (i, 0))
    blk = (_nbytes((tm, wl), F32) + 3 * _nbytes((tm, c), F32)
           + _nbytes(w2.shape, BF16) + _nbytes(a2.shape, BF16) + _nbytes(g2.shape, BF16))
    return pl.pallas_call(
        functools.partial(_lora_kernel, seq=seq, d_decay=d_decay, d_aaa=d_aaa),
        grid=(m // tm,),
        in_specs=[
            pl.BlockSpec((tm, wl), lambda i: (i, 0)),
            pl.BlockSpec((sub, wl), lambda i: (jnp.maximum(i * (tm // sub) - 1, 0), 0)),
            vec(wl), vec(c), vec(c), full(w2), full(a2), full(g2),
        ],
        out_specs=[out, out, out],
        out_shape=[jax.ShapeDtypeStruct((m, c), F32),
                   jax.ShapeDtypeStruct((m, c), BF16), jax.ShapeDtypeStruct((m, c), BF16)],
        compiler_params=pltpu.CompilerParams(
            dimension_semantics=("parallel",), vmem_limit_bytes=_vmem_limit(blk)),
        name="rwkv_lora",
    )(lora_raw, lora_raw, mu.reshape(1, wl), w0.reshape(1, c), a0.reshape(1, c), w2, a2, g2)


def _bf(x):
    return x.astype(BF16)


def _dot(a, b):
    return jnp.dot(_bf(a), _bf(b), preferred_element_type=F32)


def _dot_nt(a, b):
    return lax.dot_general(_bf(a), _bf(b), (((1,), (1,)), ((), ())), preferred_element_type=F32)


def _dot_tn(a, b):
    return lax.dot_general(_bf(a), _bf(b), (((0,), (0,)), ((), ())), preferred_element_type=F32)


def _split_dot(x, ones_bf, parts, ones_left=False):
    acc, rem = None, x
    for p in range(parts):
        hi = _bf(rem)
        ops = (ones_bf, hi) if ones_left else (hi, ones_bf)
        term = jnp.dot(*ops, preferred_element_type=F32)
        acc = term if acc is None else acc + term
        if p + 1 < parts:
            rem = rem - hi.astype(F32)
    return acc


def _wkv_kernel(r_ref, k_ref, v_ref, cum_ref, a_ref, g_ref,
                mur_ref, muk_ref, muv_ref, kk_ref, ka_ref, rk_ref, lnw_ref, lnb_ref,
                o_ref, rm_scr, yg_scr, bonus_scr, y_scr, h_scr, *, n_chunks, group):
    c_len, two = CHUNK, 2 * CHUNK
    g_rows = group * c_len
    shift_c = CHUNK.bit_length() - 1
    lane = lax.broadcasted_iota(jnp.int32, (c_len, PAIR), 1)
    m0 = (lane < RWKV_HEAD).astype(F32)
    m1 = 1.0 - m0
    row = lax.broadcasted_iota(jnp.int32, (two, two), 0)
    col = lax.broadcasted_iota(jnp.int32, (two, two), 1)
    same = (row >> shift_c) == (col >> shift_c)
    strict = same & ((row & (CHUNK - 1)) > (col & (CHUNK - 1)))
    incl = same & ((row & (CHUNK - 1)) >= (col & (CHUNK - 1)))
    eye = (row == col).astype(F32)
    head_ones = _bf(same.astype(F32))
    g_row = lax.broadcasted_iota(jnp.int32, (g_rows, PAIR), 0)
    row0 = g_row == 0
    chunk_row0 = (g_row & (CHUNK - 1)) == 0

    mu_r, mu_k, mu_v = mur_ref[...], muk_ref[...], muv_ref[...]
    k_k, k_a, r_k = kk_ref[...], ka_ref[...], rk_ref[...]

    def stack(x):
        return jnp.concatenate([x * m0, x * m1], axis=0)

    rm_scr[pl.ds(0, group * 2 * two), :] = jnp.zeros((group * 2 * two, PAIR), BF16)
    yg_scr[pl.ds(0, group * 2 * two), :] = jnp.zeros((group * 2 * two, PAIR), F32)
    h_scr[...] = jnp.zeros_like(h_scr)
    y_scr[pl.ds(0, g_rows), :] = jnp.zeros((g_rows, PAIR), F32)

    def serial_step(slot, h):
        base = pl.multiple_of(slot * (2 * two), 2 * two)
        out = jnp.dot(rm_scr[pl.ds(base, 2 * two), :], h,
                      preferred_element_type=F32) + yg_scr[pl.ds(base, 2 * two), :]
        rows = pl.ds(pl.multiple_of((slot + group) * c_len, c_len), c_len)
        y_scr[rows, :] = out[:c_len] + out[c_len:two]
        return _bf(out[two:])

    n_boundaries = 9
    serial_at = [j * n_boundaries // group for j in range(group)]
    n_groups = n_chunks // group
    ln_w, ln_b = lnw_ref[...], lnb_ref[...]
    inv_n = 1.0 / RWKV_HEAD

    def group_rows(gi):
        start = gi * g_rows
        return pl.ds(start if isinstance(gi, int) else pl.multiple_of(start, g_rows), g_rows)

    def finish(y_blocks, groups):
        rows = [group_rows(gi) for gi in groups]
        ys = [y_scr[group_rows(yb), :] for yb in y_blocks]
        ds = [y - _split_dot(y, head_ones, 1) * inv_n for y in ys]
        vs = [_split_dot(d * d, head_ones, 1) * inv_n for d in ds]
        return [((d * lax.rsqrt(var + GN_EPS) * ln_w + ln_b + bonus_scr[r, :])
                 * g_ref[r, :].astype(F32)).astype(o_ref.dtype) for r, d, var in zip(rows, ds, vs)]

    def local(i, _):
        rows = pl.ds(pl.multiple_of(i * g_rows, g_rows), g_rows)
        before = pl.ds(pl.multiple_of(jnp.maximum(i * g_rows - SUBLANES, 0), SUBLANES), SUBLANES)
        keep_prev = jnp.where(i > 0, 1.0, 0.0)
        state = [h_scr[...]]

        def boundary(k):
            for j in range(group):
                if serial_at[j] == k:
                    state[0] = serial_step(i * group + j, state[0])

        def shift(ref, mu):
            x = ref[rows, :]
            last = ref[before, :][SUBLANES - 1:, :] * keep_prev
            prev = jnp.where(row0, last, pltpu.roll(x, 1, axis=0))
            return x + (prev - x) * mu

        rs, ks, vs = shift(r_ref, mu_r), shift(k_ref, mu_k), shift(v_ref, mu_v)
        cum, a = cum_ref[rows, :], a_ref[rows, :].astype(F32)

        kk = ks * k_k
        norm = jnp.sqrt(_split_dot(kk * kk, head_ones, 1))
        kk = kk / jnp.maximum(norm, 1e-12)
        kx = ks * (1.0 + (a - 1.0) * k_a)
        b = kk * a
        bonus_scr[rows, :] = _split_dot(rs * kx * r_k, head_ones, 1) * vs
        (o_block,) = finish([i], [jnp.maximum(i - 2, 0)])

        cum_before = jnp.where(chunk_row0, 0.0, pltpu.roll(cum, 1, axis=0))
        e_neg = jnp.exp(-cum)
        at_g = -kk * jnp.exp(cum_before)
        rt_g = rs * jnp.exp(cum)
        bt_g, kt_g = _bf(b * e_neg), _bf(kx * e_neg)

        st = []
        for j in range(group):
            sl = slice(j * c_len, (j + 1) * c_len)
            c_end = cum[(j + 1) * c_len - 1:(j + 1) * c_len, :]
            e_rem = jnp.exp(c_end - cum[sl])
            at2, rt2 = stack(at_g[sl]), stack(rt_g[sl])
            st.append(dict(
                at2=at2, rt2=rt2, v2=_bf(stack(vs[sl])), d_end=jnp.exp(c_end),
                bk2=jnp.concatenate([stack(b[sl] * e_rem), stack(kx[sl] * e_rem)], axis=0),
                s=_dot_nt(jnp.concatenate([at2, rt2], axis=0),
                          jnp.concatenate([bt_g[sl], bt_g[sl], kt_g[sl], kt_g[sl]], axis=0))))
        boundary(0)
        for d in st:
            s = d.pop("s")
            x = jnp.where(strict, s[:two, :two], 0.0)
            a_ak = jnp.where(strict, s[:two, two:], 0.0)
            d["rbk"] = jnp.concatenate([jnp.where(incl, s[two:, :two], 0.0),
                                        jnp.where(incl, s[two:, two:], 0.0)], axis=1)
            d["t_sum"] = eye + x
            d["x_pow"] = _dot(x, x)
            d["z2"] = _dot(a_ak, d["v2"])
        boundary(1)
        for step in range(CHUNK.bit_length() - 3):
            for d in st:
                both = _dot(d["x_pow"], jnp.concatenate([d["t_sum"], d["x_pow"]], axis=1))
                d["t_sum"] = d["t_sum"] + both[:, :two]
                d["x_pow"] = both[:, two:]
            boundary(2 + step)
        for d in st:
            d["t_inv"] = d["t_sum"] + _dot(d["x_pow"], d["t_sum"])
        boundary(6)
        for d in st:
            pq = _dot(d["t_inv"], jnp.concatenate([_bf(d["at2"]), _bf(d["z2"])], axis=1))
            d["rhs"] = jnp.concatenate(
                [_bf(pq), jnp.concatenate([jnp.zeros_like(d["v2"]), d["v2"]], axis=1)], axis=0)
        boundary(7)
        for d in st:
            d["n1"] = _dot(d["rbk"], d["rhs"])
            d["n2"] = _dot_tn(d["bk2"], d["rhs"])
        boundary(8)
        for j, d in enumerate(st):
            n1, n2 = d["n1"], d["n2"]
            rp2 = d["rt2"] + n1[:, :two]
            m_blk = n2[:, :two] + eye * d["d_end"]
            base = pl.multiple_of(((i + 1) * group + j) * (2 * two), 2 * two)
            rm_scr[pl.ds(base, 2 * two), :] = _bf(jnp.concatenate([rp2, m_blk], axis=0))
            yg_scr[pl.ds(base, 2 * two), :] = jnp.concatenate([n1[:, two:], n2[:, two:]], axis=0)
        h_scr[...] = state[0]

        @pl.when(i >= 2)
        def _():
            o_ref[group_rows(jnp.maximum(i - 2, 0)), :] = o_block

        return 0

    lax.fori_loop(0, n_groups, local, 0)

    def drain(j, _):
        h_scr[...] = serial_step(n_chunks + j, h_scr[...])
        return 0

    lax.fori_loop(0, group, drain, 0)

    tail = list(range(max(n_groups - 2, 0), n_groups))
    for gi, o_block in zip(tail, finish([gi + 2 for gi in tail], tail)):
        o_ref[group_rows(gi), :] = o_block


def _wkv(rkv, cum, a, g, mu_rkv, k_k, k_a, r_k, ln_w, ln_b, *, batch, seq, group=8):
    m, c = cum.shape
    n_pairs = c // PAIR
    n_chunks = seq // CHUNK
    assert seq % (group * CHUNK) == 0
    col = lambda off: pl.BlockSpec((seq, PAIR), lambda b, p: (b, p + off))
    vec = lambda off: pl.BlockSpec((1, PAIR), lambda b, p: (0, p + off))
    vecs = [x.reshape(1, c) for x in (k_k, k_a, r_k, ln_w, ln_b)]
    blk = 6 * _nbytes((seq, PAIR), F32) + _nbytes((seq, PAIR), BF16)
    slot_rows = (n_chunks + group) * 4 * CHUNK
    y_rows = seq + 2 * group * CHUNK
    scratch = [pltpu.VMEM((slot_rows, PAIR), BF16),
               pltpu.VMEM((slot_rows, PAIR), F32),
               pltpu.VMEM((seq, PAIR), F32),
               pltpu.VMEM((y_rows, PAIR), F32),
               pltpu.VMEM((PAIR, PAIR), BF16)]
    scr = (_nbytes((slot_rows, PAIR), BF16) + _nbytes((slot_rows, PAIR), F32)
           + _nbytes((seq, PAIR), F32) + _nbytes((y_rows, PAIR), F32))
    return pl.pallas_call(
        functools.partial(_wkv_kernel, n_chunks=n_chunks, group=group),
        grid=(batch, n_pairs),
        in_specs=[col(0), col(n_pairs), col(2 * n_pairs), col(0), col(0), col(0),
                  vec(0), vec(n_pairs), vec(2 * n_pairs)] + [vec(0)] * 5,
        out_specs=col(0),
        out_shape=jax.ShapeDtypeStruct((m, c), BF16),
        scratch_shapes=scratch,
        compiler_params=pltpu.CompilerParams(
            dimension_semantics=("parallel", "parallel"),
            vmem_limit_bytes=int(min(VMEM_CAP, 2 * blk + scr + 8 * 1024 * 1024))),
        name="wkv7_chunked",
    )(rkv, rkv, rkv, cum, a, g, mu_rkv, mu_rkv, mu_rkv, *vecs)


def _rope_tables(seq):
    pos = jnp.arange(seq, dtype=F32)
    inv_freq = ROPE_THETA ** (-jnp.arange(0, ATT_HEAD, 2, dtype=F32) / ATT_HEAD)
    ang = pos[:, None] * inv_freq[None, :]
    cos, sin = jnp.cos(ang), jnp.sin(ang)
    return jnp.concatenate([cos, cos], axis=-1), jnp.concatenate([-sin, sin], axis=-1)


def _pad_to(x, n, axis):
    pad = n - x.shape[axis]
    if pad == 0:
        return x
    widths = [(0, 0)] * x.ndim
    widths[axis] = (0, pad)
    return jnp.pad(x, widths)


def _round_up(n, mult):
    return -(-n // mult) * mult


def _layer(x2, batch, seq, norm_mix_pre, norm_mix_post, norm_ffn_pre, norm_ffn_post, w_in, b_qkv,
           att_sinks, mu_shift, w0, w2, a0, a2, g2, k_k, k_a, r_k, ln_x_w, ln_x_b,
           w_att_branch, w_rwkv_branch, w_out, w_ffn_gate, w_ffn_up, w_ffn_down):
    d = x2.shape[1]
    n_q = att_sinks.shape[0]
    q_w = n_q * ATT_HEAD
    qkv_w = b_qkv.shape[0]
    n_kv = (qkv_w - q_w) // (2 * ATT_HEAD)
    c = w0.shape[0]
    d_decay, d_aaa, d_gate = w2.shape[0], a2.shape[0], g2.shape[0]
    lora_w = d_decay + d_aaa + d_gate
    lora_pad = _round_up(lora_w, LANES)
    ffn = w_ffn_gate.shape[1]

    c0 = qkv_w
    lora_col0 = c0 + 3 * c
    gate_col0 = lora_col0 + lora_w
    main_w = lora_col0 + lora_pad
    w_in_bf, w_gates = _cast_bf16(
        w_in, [(0, main_w, main_w), (gate_col0, 2 * d, 2 * d)], tr=64)
    mu_rkv = mu_shift[:3 * c].reshape(1, 3 * c)
    mu_lora = _pad_to(mu_shift[3 * c:], lora_pad, 0)
    g2_pad = _pad_to(_bf(g2), lora_pad - d_decay - d_aaa, 0)

    (h,) = _norm_call(x2, norm_mix_pre, emit_main=False)

    qkv = _matmul(h, w_in_bf, n_out=c0, out_dtype=BF16, bias=b_qkv)
    rkv = _matmul(h, w_in_bf, w_col0=c0, n_out=3 * c, out_dtype=F32)
    lora_raw = _matmul(h, w_in_bf, w_col0=lora_col0, n_out=lora_pad, out_dtype=F32, tn=lora_pad)
    gates = _matmul(h, w_gates, out_dtype=BF16, act="sigmoid")

    cos, sin = _rope_tables(seq)
    o_att = _attention(qkv, att_sinks.astype(F32), cos, sin, seq=seq, n_q=n_q, n_kv=n_kv)

    cum, a_gate, g_out = _lora(lora_raw, mu_lora, w0, a0, _bf(w2), _bf(a2), g2_pad,
                              seq=seq, d_decay=d_decay, d_aaa=d_aaa)
    o_rwkv = _wkv(rkv, cum, a_gate, g_out, mu_rkv, k_k, k_a, r_k.reshape(-1), ln_x_w, ln_x_b,
                  batch=batch, seq=seq)

    m_att = _matmul(o_att, _bf(w_att_branch), out_dtype=F32, tn=512, mul=gates, mul_col0=0)
    merged = _matmul(o_rwkv, _bf(w_rwkv_branch), out_dtype=BF16, tn=512, mul=gates, mul_col0=d,
                     add=m_att)
    mixed = _matmul(merged, _bf(w_out), out_dtype=BF16)
    x1, h2 = _norm_call(mixed, norm_mix_post, res=x2, g_next=norm_ffn_pre)

    tn_ffn = 512
    ffn_pad = _round_up(ffn, 2 * tn_ffn)
    (w_gate,) = _cast_bf16(w_ffn_gate, [(0, ffn, ffn_pad)], tr=128)
    (w_up,) = _cast_bf16(w_ffn_up, [(0, ffn, ffn_pad)], tr=128)
    (w_dn,) = _cast_bf16(w_ffn_down, [(0, d, d)], rows_out=ffn_pad)
    act = _matmul(h2, w_gate, w_up=w_up, out_dtype=BF16, act="swiglu", tn=tn_ffn)
    f = _matmul(act, w_dn, out_dtype=BF16, tk=ffn_pad // 4)
    (out,) = _norm_call(f, norm_ffn_post, res=x1)
    return out


def kernel(x, norm_mix_pre, norm_mix_post, norm_ffn_pre, norm_ffn_post, w_in, b_qkv, att_sinks,
           mu_shift, w0, w2, a0, a2, g2, k_k, k_a, r_k, ln_x_w, ln_x_b,
           w_att_branch, w_rwkv_branch, w_out, w_ffn_gate, w_ffn_up, w_ffn_down):
    batch, seq, d = x.shape
    x2 = x.reshape(batch * seq, d)
    for l in range(w_in.shape[0]):
        x2 = _layer(x2, batch, seq, norm_mix_pre[l], norm_mix_post[l], norm_ffn_pre[l],
                    norm_ffn_post[l], w_in[l], b_qkv[l], att_sinks[l], mu_shift[l], w0[l], w2[l],
                    a0[l], a2[l], g2[l], k_k[l], k_a[l], r_k[l], ln_x_w[l], ln_x_b[l],
                    w_att_branch[l], w_rwkv_branch[l], w_out[l],
                    w_ffn_gate[l], w_ffn_up[l], w_ffn_down[l])
    return x2.reshape(batch, seq, d)
```

```python
import functools
import math

import jax
import jax.numpy as jnp
from jax import lax
from jax.experimental import pallas as pl
from jax.experimental.pallas import tpu as pltpu

F32 = jnp.float32
BF16 = jnp.bfloat16

LANES = 128
SUBLANES = 8
V7X_VMEM_BYTES = 64 * 1024 * 1024
VMEM_CAP = V7X_VMEM_BYTES - 6 * 1024 * 1024

ATT_HEAD = 128
ATT_GROUP = 4
WINDOW = 128
ROPE_THETA = 10000.0
NEG_INF = -1e30
RWKV_HEAD = 64
CHUNK = 64
PAIR = 2 * RWKV_HEAD
GN_EPS = 64e-5
RMS_EPS = 1e-6
DECAY_SCALE = math.exp(-0.5)


def _vmem_limit(block_bytes):
    return int(min(VMEM_CAP, 2 * block_bytes + 16 * 1024 * 1024))


def _nbytes(shape, dtype):
    n = 1
    for s in shape:
        n *= s
    return n * jnp.dtype(dtype).itemsize


def _cast_kernel(x_ref, *o_refs, windows, n_row_blocks, pad_rows):
    i = pl.program_id(0)
    x = x_ref[...]
    for o_ref, (c0, width) in zip(o_refs, windows):
        o_ref[:, :width] = x[:, c0:c0 + width].astype(o_ref.dtype)
        if o_ref.shape[1] > width:
            o_ref[:, width:] = jnp.zeros((o_ref.shape[0], o_ref.shape[1] - width), o_ref.dtype)
        if pad_rows:
            @pl.when(i >= n_row_blocks)
            def _():
                o_ref[...] = jnp.zeros_like(o_ref)


def _cast_bf16(w, windows, *, rows_out=None, tr=256):
    rows, cols = w.shape
    rows_out = rows if rows_out is None else rows_out
    tr = min(tr, rows)
    assert rows % tr == 0 and rows_out % tr == 0
    n_in = rows // tr
    blk = _nbytes((tr, cols), F32) + sum(_nbytes((tr, wp), BF16) for _, _, wp in windows)
    return pl.pallas_call(
        functools.partial(_cast_kernel, windows=[(c0, wd) for c0, wd, _ in windows],
                          n_row_blocks=n_in, pad_rows=rows_out > rows),
        grid=(rows_out // tr,),
        in_specs=[pl.BlockSpec((tr, cols), lambda i: (jnp.minimum(i, n_in - 1), 0))],
        out_specs=[pl.BlockSpec((tr, wp), lambda i: (i, 0)) for _, _, wp in windows],
        out_shape=[jax.ShapeDtypeStruct((rows_out, wp), BF16) for _, _, wp in windows],
        compiler_params=pltpu.CompilerParams(
            dimension_semantics=("parallel",), vmem_limit_bytes=_vmem_limit(blk)),
        name="cast_bf16",
    )(w)


def _rms(x, g):
    return x * lax.rsqrt(jnp.mean(x * x, axis=-1, keepdims=True) + RMS_EPS) * g


def _norm_kernel(*refs, has_res, emit_main, second_norm):
    it = iter(refs)
    x_ref, g_ref = next(it), next(it)
    res_ref = next(it) if has_res else None
    g2_ref = next(it) if second_norm else None
    main_ref = next(it) if emit_main else None
    next_ref = next(it) if (second_norm or not emit_main) else None
    y = _rms(x_ref[...].astype(F32), g_ref[...])
    if has_res:
        y = res_ref[...] + y
    if emit_main:
        main_ref[...] = y
    if second_norm:
        next_ref[...] = _rms(y, g2_ref[...]).astype(next_ref.dtype)
    elif not emit_main:
        next_ref[...] = y.astype(next_ref.dtype)


def _norm_call(x, g, res=None, g_next=None, *, emit_main=True, tm=256):
    m, d = x.shape
    tm = min(tm, m)
    second_norm = g_next is not None
    assert emit_main or not second_norm
    row = pl.BlockSpec((tm, d), lambda i: (i, 0))
    vec = pl.BlockSpec((1, d), lambda i: (0, 0))
    args, specs = [x, g.reshape(1, d)], [row, vec]
    if res is not None:
        args.append(res); specs.append(row)
    if second_norm:
        args.append(g_next.reshape(1, d)); specs.append(vec)
    out_shape, out_specs = [], []
    if emit_main:
        out_shape.append(jax.ShapeDtypeStruct((m, d), F32)); out_specs.append(row)
    if second_norm or not emit_main:
        out_shape.append(jax.ShapeDtypeStruct((m, d), BF16)); out_specs.append(row)
    blk = _nbytes((tm, d), F32) * (len(args) + len(out_shape))
    outs = pl.pallas_call(
        functools.partial(_norm_kernel, has_res=res is not None, emit_main=emit_main,
                          second_norm=second_norm),
        grid=(m // tm,),
        in_specs=specs, out_specs=out_specs, out_shape=out_shape,
        compiler_params=pltpu.CompilerParams(
            dimension_semantics=("parallel",), vmem_limit_bytes=_vmem_limit(blk)),
        name="rmsnorm",
    )(*args)
    return outs


def _sigmoid(x):
    return 0.5 * jnp.tanh(0.5 * x) + 0.5


def _mm_kernel(*refs, nk, has_bias, act, has_mul, has_add):
    it = iter(refs)
    a_ref, w_ref = next(it), next(it)
    up_ref = next(it) if act == "swiglu" else None
    bias_ref = next(it) if has_bias else None
    mul_ref = next(it) if has_mul else None
    add_ref = next(it) if has_add else None
    o_ref = next(it)
    acc_ref = next(it) if nk > 1 else None

    def product():
        return jnp.dot(a_ref[...], w_ref[...], preferred_element_type=F32)

    def finish(y):
        if has_bias:
            y = y + bias_ref[...]
        if act == "sigmoid":
            y = _sigmoid(y)
        elif act == "swiglu":
            up = jnp.dot(a_ref[...], up_ref[...], preferred_element_type=F32)
            y = y * _sigmoid(y) * up
        if has_mul:
            y = y * mul_ref[...].astype(F32)
        if has_add:
            y = y + add_ref[...].astype(F32)
        o_ref[...] = y.astype(o_ref.dtype)

    if nk == 1:
        finish(product())
    else:
        k = pl.program_id(2)

        @pl.when(k == 0)
        def _():
            acc_ref[...] = product()

        @pl.when(jnp.logical_and(k > 0, k < nk - 1))
        def _():
            acc_ref[...] += product()

        @pl.when(k == nk - 1)
        def _():
            finish(acc_ref[...] + product())


def _matmul(a, w, *, out_dtype, tm=1024, tn=1024, tk=None, bias=None, act=None, w_up=None,
            mul=None, mul_col0=0, add=None, w_col0=0, n_out=None):
    m, kdim = a.shape
    n_out = w.shape[1] if n_out is None else n_out
    tm = min(tm, m)
    tk = kdim if tk is None else tk
    nk = kdim // tk
    assert m % tm == 0 and kdim % tk == 0
    assert n_out % tn == 0 and tn % LANES == 0 and w_col0 % tn == 0
    assert (act == "swiglu") == (w_up is not None) and (w_up is None or nk == 1)
    grid = (m // tm, n_out // tn, nk)
    w_off = w_col0 // tn

    args = [a, w]
    w_spec = pl.BlockSpec((tk, tn), lambda i, j, k: (k, j + w_off))
    specs = [pl.BlockSpec((tm, tk), lambda i, j, k: (i, k)), w_spec]
    blk = _nbytes((tm, tk), a.dtype) + _nbytes((tk, tn), w.dtype)
    if w_up is not None:
        args.append(w_up)
        specs.append(w_spec)
        blk += _nbytes((tk, tn), w_up.dtype)
    if bias is not None:
        args.append(bias.reshape(1, n_out).astype(F32))
        specs.append(pl.BlockSpec((1, tn), lambda i, j, k: (0, j)))
    if mul is not None:
        assert mul_col0 % tn == 0
        off = mul_col0 // tn
        args.append(mul)
        specs.append(pl.BlockSpec((tm, tn), lambda i, j, k: (i, j + off)))
        blk += _nbytes((tm, tn), mul.dtype)
    if add is not None:
        args.append(add)
        specs.append(pl.BlockSpec((tm, tn), lambda i, j, k: (i, j)))
        blk += _nbytes((tm, tn), add.dtype)
    blk += _nbytes((tm, tn), out_dtype)
    scratch = [pltpu.VMEM((tm, tn), F32)] if nk > 1 else []
    extra = _nbytes((tm, tn), F32) * (4 if (nk > 1 or w_up is not None) else 3)

    return pl.pallas_call(
        functools.partial(_mm_kernel, nk=nk, has_bias=bias is not None, act=act,
                          has_mul=mul is not None, has_add=add is not None),
        grid=grid,
        in_specs=specs,
        out_specs=pl.BlockSpec((tm, tn), lambda i, j, k: (i, j)),
        out_shape=jax.ShapeDtypeStruct((m, n_out), out_dtype),
        scratch_shapes=scratch,
        compiler_params=pltpu.CompilerParams(
            dimension_semantics=("parallel", "parallel", "arbitrary"),
            vmem_limit_bytes=int(min(VMEM_CAP, 2 * blk + extra + 4 * 1024 * 1024))),
        name="matmul_" + (act or "plain"),
    )(*args)


def _attn_kernel(sinks_ref, q_ref, kc_ref, kp_ref, vc_ref, vp_ref,
                 cosq_ref, sinq_ref, cosc_ref, sinc_ref, cosp_ref, sinp_ref, o_ref, bias_scr,
                 *, nb, n_kv):
    i = pl.program_id(0)
    first = (i % nb) == 0
    blk = WINDOW

    def rope(t, cos, sin):
        t = t.astype(F32)
        return t * cos + pltpu.roll(t, ATT_HEAD // 2, axis=1) * sin

    cos_q, sin_q = cosq_ref[...], sinq_ref[...]
    cos_c, sin_c = cosc_ref[...], sinc_ref[...]
    cos_p, sin_p = cosp_ref[...], sinp_ref[...]

    qi = lax.broadcasted_iota(jnp.int32, (blk, 2 * blk), 0)
    kj = lax.broadcasted_iota(jnp.int32, (blk, 2 * blk), 1)
    lo = jnp.where(first, blk, 0)
    valid = (kj > qi) & (kj <= qi + blk) & (kj >= lo)
    bias_scr[...] = jnp.where(valid, 0.0, NEG_INF)
    ones = jnp.ones((2 * blk, ATT_HEAD), BF16)

    def scores(h):
        cs = slice(h * ATT_HEAD, (h + 1) * ATT_HEAD)
        k_band = jnp.concatenate(
            [rope(kp_ref[:, cs], cos_p, sin_p), rope(kc_ref[:, cs], cos_c, sin_c)],
            axis=0).astype(BF16)
        q_all = jnp.concatenate(
            [rope(q_ref[:, hq * ATT_HEAD:(hq + 1) * ATT_HEAD], cos_q, sin_q).astype(BF16)
             for hq in range(h * ATT_GROUP, (h + 1) * ATT_GROUP)], axis=0)
        return lax.dot_general(q_all, k_band, (((1,), (1,)), ((), ())),
                               preferred_element_type=F32)

    def finalize(h, pv, sink_parts):
        for g in range(ATT_GROUP):
            hq = h * ATT_GROUP + g
            pv_g = pv[g * blk:(g + 1) * blk]
            o_g = pv_g[:, :ATT_HEAD] / (pv_g[:, ATT_HEAD:] + sink_parts[g])
            o_ref[:, hq * ATT_HEAD:(hq + 1) * ATT_HEAD] = o_g.astype(o_ref.dtype)

    s_next, pending = scores(0), None
    for h in range(n_kv):
        s = s_next
        if h + 1 < n_kv:
            s_next = scores(h + 1)
        cs = slice(h * ATT_HEAD, (h + 1) * ATT_HEAD)
        v_ext = jnp.concatenate(
            [jnp.concatenate([vp_ref[:, cs], vc_ref[:, cs]], axis=0), ones], axis=1)
        p_parts, sink_parts = [], []
        for g in range(ATT_GROUP):
            sink = sinks_ref[h * ATT_GROUP + g]
            s_g = s[g * blk:(g + 1) * blk] + bias_scr[...]
            m_g = jnp.maximum(jnp.max(s_g, axis=-1, keepdims=True), sink)
            p_parts.append(jnp.exp(s_g - m_g).astype(BF16))
            sink_parts.append(jnp.exp(sink - m_g))
        pv = jnp.dot(jnp.concatenate(p_parts, axis=0), v_ext,
                     preferred_element_type=F32)
        if pending is not None:
            finalize(*pending)
        pending = (h, pv, sink_parts)
    finalize(*pending)


def _attention(qkv, sinks, cos, sin, *, seq, n_q, n_kv):
    m = qkv.shape[0]
    nb = seq // WINDOW
    q_w, kv_w = n_q * ATT_HEAD, n_kv * ATT_HEAD
    k_blk0, v_blk0 = q_w // kv_w, q_w // kv_w + 1
    prev = lambda i: jnp.maximum(i - 1, 0)
    scale = ATT_HEAD ** -0.5
    tab_c =pl.BlockSpec((WINDOW, ATT_HEAD), lambda i: (i % nb, 0))
    tab_p = pl.BlockSpec((WINDOW, ATT_HEAD), lambda i: (jnp.maximum(i % nb - 1, 0), 0))
    blk = (_nbytes((WINDOW, q_w), BF16) * 2 + 4 * _nbytes((WINDOW, kv_w), BF16)
           + 4 * _nbytes((WINDOW, ATT_HEAD), F32))
    return pl.pallas_call(
        functools.partial(_attn_kernel, nb=nb, n_kv=n_kv),
        grid=(m // WINDOW,),
        in_specs=[
            pl.BlockSpec(memory_space=pltpu.SMEM),
            pl.BlockSpec((WINDOW, q_w), lambda i: (i, 0)),
            pl.BlockSpec((WINDOW, kv_w), lambda i: (i, k_blk0)),
            pl.BlockSpec((WINDOW, kv_w), lambda i: (prev(i), k_blk0)),
            pl.BlockSpec((WINDOW, kv_w), lambda i: (i, v_blk0)),
            pl.BlockSpec((WINDOW, kv_w), lambda i: (prev(i), v_blk0)),
            tab_c, tab_c, tab_c, tab_c, tab_p, tab_p,
        ],
        out_specs=pl.BlockSpec((WINDOW, q_w), lambda i: (i, 0)),
        out_shape=jax.ShapeDtypeStruct((m, q_w), BF16),
        scratch_shapes=[pltpu.VMEM((WINDOW, 2 * WINDOW), F32)],
        compiler_params=pltpu.CompilerParams(
            dimension_semantics=("parallel",), vmem_limit_bytes=_vmem_limit(blk)),
        name="swa_attention",
    )(sinks, qkv, qkv, qkv, qkv, qkv, cos * scale, sin * scale, cos, sin, cos, sin)


def _lora_kernel(x_ref, prev_ref, mu_ref, w0_ref, a0_ref, w2_ref, a2_ref, g2_ref,
                 cum_ref, a_ref, g_ref, *, seq, d_decay, d_aaa):
    i = pl.program_id(0)
    tm = x_ref.shape[0]
    x = x_ref[...]
    last_prev = prev_ref[prev_ref.shape[0] - 1:, :]
    last_prev = jnp.where((i * tm) % seq == 0, 0.0, last_prev)
    row = lax.broadcasted_iota(jnp.int32, x.shape, 0)
    prev = jnp.where(row == 0, last_prev, pltpu.roll(x, 1, axis=0))
    xs = x + (prev - x) * mu_ref[...]
    p_w = jnp.tanh(xs[:, :d_decay]).astype(BF16)
    p_a = xs[:, d_decay:d_decay + d_aaa].astype(BF16)
    p_g = _sigmoid(xs[:, d_decay + d_aaa:]).astype(BF16)
    z = w0_ref[...] + jnp.dot(p_w, w2_ref[...], preferred_element_type=F32)
    log_decay = -DECAY_SCALE * _sigmoid(z)
    shift_c = CHUNK.bit_length() - 1
    tr = lax.broadcasted_iota(jnp.int32, (tm, tm), 0)
    tc = lax.broadcasted_iota(jnp.int32, (tm, tm), 1)
    tri_ones = (((tr >> shift_c) == (tc >> shift_c)) & (tr >= tc)).astype(F32).astype(BF16)
    cum_ref[...] = _split_dot(log_decay, tri_ones, 2, ones_left=True)
    a_ref[...] = _sigmoid(
        a0_ref[...] + jnp.dot(p_a, a2_ref[...], preferred_element_type=F32)).astype(a_ref.dtype)
    g_ref[...] = jnp.dot(p_g, g2_ref[...], preferred_element_type=F32).astype(g_ref.dtype)


def _lora(lora_raw, mu, w0, a0, w2, a2, g2, *, seq, d_decay, d_aaa, tm=256):
    m, wl = lora_raw.shape
    c = w2.shape[1]
    tm = min(tm, seq)
    assert seq % tm == 0 and tm % CHUNK == 0
    sub = SUBLANES
    vec = lambda n: pl.BlockSpec((1, n), lambda i: (0, 0))
    full = lambda a: pl.BlockSpec(a.shape, lambda i: (0, 0))
    out = pl.BlockSpec((tm, c), lambda i: (i, 0))
    blk = (_nbytes((tm, wl), F32) + 3 * _nbytes((tm, c), F32)
           + _nbytes(w2.shape, BF16) + _nbytes(a2.shape, BF16) + _nbytes(g2.shape, BF16))
    return pl.pallas_call(
        functools.partial(_lora_kernel, seq=seq, d_decay=d_decay, d_aaa=d_aaa),
        grid=(m // tm,),
        in_specs=[
            pl.BlockSpec((tm, wl), lambda i: (i, 0)),
            pl.BlockSpec((sub, wl), lambda i: (jnp.maximum(i * (tm // sub) - 1, 0), 0)),
            vec(wl), vec(c), vec(c), full(w2), full(a2), full(g2),
        ],
        out_specs=[out, out, out],
        out_shape=[jax.ShapeDtypeStruct((m, c), F32),
                   jax.ShapeDtypeStruct((m, c), BF16), jax.ShapeDtypeStruct((m, c), BF16)],
        compiler_params=pltpu.CompilerParams(
            dimension_semantics=("parallel",), vmem_limit_bytes=_vmem_limit(blk)),
        name="rwkv_lora",
    )(lora_raw, lora_raw, mu.reshape(1, wl), w0.reshape(1, c), a0.reshape(1, c), w2, a2, g2)


def _bf(x):
    return x.astype(BF16)


def _dot(a, b):
    return jnp.dot(_bf(a), _bf(b), preferred_element_type=F32)


def _dot_nt(a, b):
    return lax.dot_general(_bf(a), _bf(b), (((1,), (1,)), ((), ())), preferred_element_type=F32)


def _dot_tn(a, b):
    return lax.dot_general(_bf(a), _bf(b), (((0,), (0,)), ((), ())), preferred_element_type=F32)


def _split_dot(x, ones_bf, parts, ones_left=False):
    acc, rem = None, x
    for p in range(parts):
        hi = _bf(rem)
        ops = (ones_bf, hi) if ones_left else (hi, ones_bf)
        term = jnp.dot(*ops, preferred_element_type=F32)
        acc = term if acc is None else acc + term
        if p + 1 < parts:
            rem = rem - hi.astype(F32)
    return acc


def _wkv_kernel(r_ref, k_ref, v_ref, cum_ref, a_ref, g_ref,
                mur_ref, muk_ref, muv_ref, kk_ref, ka_ref, rk_ref, lnw_ref, lnb_ref,
                o_ref, rm_scr, yg_scr, bonus_scr, y_scr, h_scr, *, n_chunks, group):
    c_len, two = CHUNK, 2 * CHUNK
    g_rows = group * c_len
    shift_c = CHUNK.bit_length() - 1
    lane = lax.broadcasted_iota(jnp.int32, (c_len, PAIR), 1)
    m0 = (lane < RWKV_HEAD).astype(F32)
    m1 = 1.0 - m0
    row = lax.broadcasted_iota(jnp.int32, (two, two), 0)
    col = lax.broadcasted_iota(jnp.int32, (two, two), 1)
    same = (row >> shift_c) == (col >> shift_c)
    strict = same & ((row & (CHUNK - 1)) > (col & (CHUNK - 1)))
    incl = same & ((row & (CHUNK - 1)) >= (col & (CHUNK - 1)))
    eye = (row == col).astype(F32)
    head_ones = _bf(same.astype(F32))
    g_row = lax.broadcasted_iota(jnp.int32, (g_rows, PAIR), 0)
    row0 = g_row == 0
    chunk_row0 = (g_row & (CHUNK - 1)) == 0

    mu_r, mu_k, mu_v = mur_ref[...], muk_ref[...], muv_ref[...]
    k_k, k_a, r_k = kk_ref[...], ka_ref[...], rk_ref[...]

    def stack(x):
        return jnp.concatenate([x * m0, x * m1], axis=0)

    rm_scr[pl.ds(0, group * 2 * two), :] = jnp.zeros((group * 2 * two, PAIR), BF16)
    yg_scr[pl.ds(0, group * 2 * two), :] = jnp.zeros((group * 2 * two, PAIR), F32)
    h_scr[...] = jnp.zeros_like(h_scr)
    y_scr[pl.ds(0, g_rows), :] = jnp.zeros((g_rows, PAIR), F32)

    def serial_step(slot, h):
        base = pl.multiple_of(slot * (2 * two), 2 * two)
        out = jnp.dot(rm_scr[pl.ds(base, 2 * two), :], h,
                      preferred_element_type=F32) + yg_scr[pl.ds(base, 2 * two), :]
        rows = pl.ds(pl.multiple_of((slot + group) * c_len, c_len), c_len)
        y_scr[rows, :] = out[:c_len] + out[c_len:two]
        return _bf(out[two:])

    n_boundaries = 9
    serial_at = [j * n_boundaries // group for j in range(group)]
    n_groups = n_chunks // group
    ln_w, ln_b = lnw_ref[...], lnb_ref[...]
    inv_n = 1.0 / RWKV_HEAD

    def group_rows(gi):
        start = gi * g_rows
        return pl.ds(start if isinstance(gi, int) else pl.multiple_of(start, g_rows), g_rows)

    def finish(y_blocks, groups):
        rows = [group_rows(gi) for gi in groups]
        ys = [y_scr[group_rows(yb), :] for yb in y_blocks]
        ds = [y - _split_dot(y, head_ones, 1) * inv_n for y in ys]
        vs = [_split_dot(d * d, head_ones, 1) * inv_n for d in ds]
        return [((d * lax.rsqrt(var + GN_EPS) * ln_w + ln_b + bonus_scr[r, :])
                 * g_ref[r, :].astype(F32)).astype(o_ref.dtype) for r, d, var in zip(rows, ds, vs)]

    def local(i, _):
        rows = pl.ds(pl.multiple_of(i * g_rows, g_rows), g_rows)
        before = pl.ds(pl.multiple_of(jnp.maximum(i * g_rows - SUBLANES, 0), SUBLANES), SUBLANES)
        keep_prev = jnp.where(i > 0, 1.0, 0.0)
        state = [h_scr[...]]

        def boundary(k):
            for j in range(group):
                if serial_at[j] == k:
                    state[0] = serial_step(i * group + j, state[0])

        def shift(ref, mu):
            x = ref[rows, :]
            last = ref[before, :][SUBLANES - 1:, :] * keep_prev
            prev = jnp.where(row0, last, pltpu.roll(x, 1, axis=0))
            return x + (prev - x) * mu

        rs, ks, vs = shift(r_ref, mu_r), shift(k_ref, mu_k), shift(v_ref, mu_v)
        cum, a = cum_ref[rows, :], a_ref[rows, :].astype(F32)

        kk = ks * k_k
        norm = jnp.sqrt(_split_dot(kk * kk, head_ones, 1))
        kk = kk / jnp.maximum(norm, 1e-12)
        kx = ks * (1.0 + (a - 1.0) * k_a)
        b = kk * a
        bonus_scr[rows, :] = _split_dot(rs * kx * r_k, head_ones, 1) * vs
        (o_block,) = finish([i], [jnp.maximum(i - 2, 0)])

        cum_before = jnp.where(chunk_row0, 0.0, pltpu.roll(cum, 1, axis=0))
        e_neg = jnp.exp(-cum)
        at_g = -kk * jnp.exp(cum_before)
        rt_g = rs * jnp.exp(cum)
        bt_g, kt_g = _bf(b * e_neg), _bf(kx * e_neg)

        st = []
        for j in range(group):
            sl = slice(j * c_len, (j + 1) * c_len)
            c_end = cum[(j + 1) * c_len - 1:(j + 1) * c_len, :]
            e_rem = jnp.exp(c_end - cum[sl])
            at2, rt2 = stack(at_g[sl]), stack(rt_g[sl])
            st.append(dict(
                at2=at2, rt2=rt2, v2=_bf(stack(vs[sl])), d_end=jnp.exp(c_end),
                bk2=jnp.concatenate([stack(b[sl] * e_rem), stack(kx[sl] * e_rem)], axis=0),
                s=_dot_nt(jnp.concatenate([at2, rt2], axis=0),
                          jnp.concatenate([bt_g[sl], bt_g[sl], kt_g[sl], kt_g[sl]], axis=0))))
        boundary(0)
        for d in st:
            s = d.pop("s")
            x = jnp.where(strict, s[:two, :two], 0.0)
            a_ak = jnp.where(strict, s[:two, two:], 0.0)
            d["rbk"] = jnp.concatenate([jnp.where(incl, s[two:, :two], 0.0),
                                        jnp.where(incl, s[two:, two:], 0.0)], axis=1)
            d["t_sum"] = eye + x
            d["x_pow"] = _dot(x, x)
            d["z2"] = _dot(a_ak, d["v2"])
        boundary(1)
        for step in range(CHUNK.bit_length() - 3):
            for d in st:
                both = _dot(d["x_pow"], jnp.concatenate([d["t_sum"], d["x_pow"]], axis=1))
                d["t_sum"] = d["t_sum"] + both[:, :two]
                d["x_pow"] = both[:, two:]
            boundary(2 + step)
        for d in st:
            d["t_inv"] = d["t_sum"] + _dot(d["x_pow"], d["t_sum"])
        boundary(6)
        for d in st:
            pq = _dot(d["t_inv"], jnp.concatenate([_bf(d["at2"]), _bf(d["z2"])], axis=1))
            d["rhs"] = jnp.concatenate(
                [_bf(pq), jnp.concatenate([jnp.zeros_like(d["v2"]), d["v2"]], axis=1)], axis=0)
        boundary(7)
        for d in st:
            d["n1"] = _dot(d["rbk"], d["rhs"])
            d["n2"] = _dot_tn(d["bk2"], d["rhs"])
        boundary(8)
        for j, d in enumerate(st):
            n1, n2 = d["n1"], d["n2"]
            rp2 = d["rt2"] + n1[:, :two]
            m_blk = n2[:, :two] + eye * d["d_end"]
            base = pl.multiple_of(((i + 1) * group + j) * (2 * two), 2 * two)
            rm_scr[pl.ds(base, 2 * two), :] = _bf(jnp.concatenate([rp2, m_blk], axis=0))
            yg_scr[pl.ds(base, 2 * two), :] = jnp.concatenate([n1[:, two:], n2[:, two:]], axis=0)
        h_scr[...] = state[0]

        @pl.when(i >= 2)
        def _():
            o_ref[group_rows(jnp.maximum(i - 2, 0)), :] = o_block

        return 0

    lax.fori_loop(0, n_groups, local, 0)

    def drain(j, _):
        h_scr[...] = serial_step(n_chunks + j, h_scr[...])
        return 0

    lax.fori_loop(0, group, drain, 0)

    tail = list(range(max(n_groups - 2, 0), n_groups))
    for gi, o_block in zip(tail, finish([gi + 2 for gi in tail], tail)):
        o_ref[group_rows(gi), :] = o_block


def _wkv(rkv, cum, a, g, mu_rkv, k_k, k_a, r_k, ln_w, ln_b, *, batch, seq, group=8):
    m, c = cum.shape
    n_pairs = c // PAIR
    n_chunks = seq // CHUNK
    assert seq % (group * CHUNK) == 0
    col = lambda off: pl.BlockSpec((seq, PAIR), lambda b, p: (b, p + off))
    vec = lambda off: pl.BlockSpec((1, PAIR), lambda b, p: (0, p + off))
    vecs = [x.reshape(1, c) for x in (k_k, k_a, r_k, ln_w, ln_b)]
    blk = 6 * _nbytes((seq, PAIR), F32) + _nbytes((seq, PAIR), BF16)
    slot_rows = (n_chunks + group) * 4 * CHUNK
    y_rows = seq + 2 * group * CHUNK
    scratch = [pltpu.VMEM((slot_rows, PAIR), BF16),
               pltpu.VMEM((slot_rows, PAIR), F32),
               pltpu.VMEM((seq, PAIR), F32),
               pltpu.VMEM((y_rows, PAIR), F32),
               pltpu.VMEM((PAIR, PAIR), BF16)]
    scr = (_nbytes((slot_rows, PAIR), BF16) + _nbytes((slot_rows, PAIR), F32)
           + _nbytes((seq, PAIR), F32) + _nbytes((y_rows, PAIR), F32))
    return pl.pallas_call(
        functools.partial(_wkv_kernel, n_chunks=n_chunks, group=group),
        grid=(batch, n_pairs),
        in_specs=[col(0), col(n_pairs), col(2 * n_pairs), col(0), col(0), col(0),
                  vec(0), vec(n_pairs), vec(2 * n_pairs)] + [vec(0)] * 5,
        out_specs=col(0),
        out_shape=jax.ShapeDtypeStruct((m, c), BF16),
        scratch_shapes=scratch,
        compiler_params=pltpu.CompilerParams(
            dimension_semantics=("parallel", "parallel"),
            vmem_limit_bytes=int(min(VMEM_CAP, 2 * blk + scr + 8 * 1024 * 1024))),
        name="wkv7_chunked",
    )(rkv, rkv, rkv, cum, a, g, mu_rkv, mu_rkv, mu_rkv, *vecs)


def _rope_tables(seq):
    pos = jnp.arange(seq, dtype=F32)
    inv_freq = ROPE_THETA ** (-jnp.arange(0, ATT_HEAD, 2, dtype=F32) / ATT_HEAD)
    ang = pos[:, None] * inv_freq[None, :]
    cos, sin = jnp.cos(ang), jnp.sin(ang)
    return jnp.concatenate([cos, cos], axis=-1), jnp.concatenate([-sin, sin], axis=-1)


def _pad_to(x, n, axis):
    pad = n - x.shape[axis]
    if pad == 0:
        return x
    widths = [(0, 0)] * x.ndim
    widths[axis] = (0, pad)
    return jnp.pad(x, widths)


def _round_up(n, mult):
    return -(-n // mult) * mult


def _layer(x2, batch, seq, norm_mix_pre, norm_mix_post, norm_ffn_pre, norm_ffn_post, w_in, b_qkv,
           att_sinks, mu_shift, w0, w2, a0, a2, g2, k_k, k_a, r_k, ln_x_w, ln_x_b,
           w_att_branch, w_rwkv_branch, w_out, w_ffn_gate, w_ffn_up, w_ffn_down):
    d = x2.shape[1]
    n_q = att_sinks.shape[0]
    q_w = n_q * ATT_HEAD
    qkv_w = b_qkv.shape[0]
    n_kv = (qkv_w - q_w) // (2 * ATT_HEAD)
    c = w0.shape[0]
    d_decay, d_aaa, d_gate = w2.shape[0], a2.shape[0], g2.shape[0]
    lora_w = d_decay + d_aaa + d_gate
    lora_pad = _round_up(lora_w, LANES)
    ffn = w_ffn_gate.shape[1]

    c0 = qkv_w
    lora_col0 = c0 + 3 * c
    w_in_bf = _bf(w_in)
    w_gates = w_in_bf[:, lora_col0 + lora_w:]
    mu_rkv = mu_shift[:3 * c].reshape(1, 3 * c)
    mu_lora = _pad_to(mu_shift[3 * c:], lora_pad, 0)
    g2_pad = _pad_to(_bf(g2), lora_pad - d_decay - d_aaa, 0)

    (h,) = _norm_call(x2, norm_mix_pre, emit_main=False)

    qkv = _matmul(h, w_in_bf, n_out=c0, out_dtype=BF16, bias=b_qkv)
    rkv = _matmul(h, w_in_bf, w_col0=c0, n_out=3 * c, out_dtype=F32)
    lora_raw = _matmul(h, w_in_bf, w_col0=lora_col0, n_out=lora_pad, out_dtype=F32, tn=lora_pad)
    gates = _matmul(h, w_gates, out_dtype=BF16, act="sigmoid")

    cos, sin = _rope_tables(seq)
    o_att = _attention(qkv, att_sinks.astype(F32), cos, sin, seq=seq, n_q=n_q, n_kv=n_kv)

    cum, a_gate, g_out = _lora(lora_raw, mu_lora, w0, a0, _bf(w2), _bf(a2), g2_pad,
                              seq=seq, d_decay=d_decay, d_aaa=d_aaa)
    o_rwkv = _wkv(rkv, cum, a_gate, g_out, mu_rkv, k_k, k_a, r_k.reshape(-1), ln_x_w, ln_x_b,
                  batch=batch, seq=seq)

    m_att = _matmul(o_att, _bf(w_att_branch), out_dtype=F32, tn=512, mul=gates, mul_col0=0)
    merged = _matmul(o_rwkv, _bf(w_rwkv_branch), out_dtype=BF16, tn=512, mul=gates, mul_col0=d,
                     add=m_att)
    mixed = _matmul(merged, _bf(w_out), out_dtype=BF16)
    x1, h2 = _norm_call(mixed, norm_mix_post, res=x2, g_next=norm_ffn_pre)

    tn_ffn = 512
    ffn_pad = _round_up(ffn, 2 * tn_ffn)
    (w_gate,) = _cast_bf16(w_ffn_gate, [(0, ffn, ffn_pad)], tr=128)
    (w_up,) = _cast_bf16(w_ffn_up, [(0, ffn, ffn_pad)], tr=128)
    (w_dn,) = _cast_bf16(w_ffn_down, [(0, d, d)], rows_out=ffn_pad)
    act = _matmul(h2, w_gate, w_up=w_up, out_dtype=BF16, act="swiglu", tn=tn_ffn)
    f = _matmul(act, w_dn, out_dtype=BF16, tk=ffn_pad // 4)
    (out,) = _norm_call(f, norm_ffn_post, res=x1)
    return out


def kernel(x, norm_mix_pre, norm_mix_post, norm_ffn_pre, norm_ffn_post, w_in, b_qkv, att_sinks,
           mu_shift, w0, w2, a0, a2, g2, k_k, k_a, r_k, ln_x_w, ln_x_b,
           w_att_branch, w_rwkv_branch, w_out, w_ffn_gate, w_ffn_up, w_ffn_down):
    batch, seq, d = x.shape
    x2 = x.reshape(batch * seq, d)
    for l in range(w_in.shape[0]):
        x2 = _layer(x2, batch, seq, norm_mix_pre[l], norm_mix_post[l], norm_ffn_pre[l],
                    norm_ffn_post[l], w_in[l], b_qkv[l], att_sinks[l], mu_shift[l], w0[l], w2[l],
                    a0[l], a2[l], g2[l], k_k[l], k_a[l], r_k[l], ln_x_w[l], ln_x_b[l],
                    w_att_branch[l], w_rwkv_branch[l], w_out[l],
                    w_ffn_gate[l], w_ffn_up[l], w_ffn_down[l])
    return x2.reshape(batch, seq, d)
```

```python
import functools
import math

import jax
import jax.numpy as jnp
from jax import lax
from jax.experimental import pallas as pl
from jax.experimental.pallas import tpu as pltpu

F32 = jnp.float32
BF16 = jnp.bfloat16

LANES = 128
SUBLANES = 8
V7X_VMEM_BYTES = 64 * 1024 * 1024
VMEM_CAP = V7X_VMEM_BYTES - 6 * 1024 * 1024

ATT_HEAD = 128
ATT_GROUP = 4
WINDOW = 128
ROPE_THETA = 10000.0
NEG_INF = -1e30
RWKV_HEAD = 64
CHUNK = 64
PAIR = 2 * RWKV_HEAD
GN_EPS = 64e-5
RMS_EPS = 1e-6
DECAY_SCALE = math.exp(-0.5)


def _vmem_limit(block_bytes):
    return int(min(VMEM_CAP, 2 * block_bytes + 16 * 1024 * 1024))


def _nbytes(shape, dtype):
    n = 1
    for s in shape:
        n *= s
    return n * jnp.dtype(dtype).itemsize


def _cast_kernel(x_ref, *o_refs, windows, n_row_blocks, pad_rows):
    i = pl.program_id(0)
    x = x_ref[...]
    for o_ref, (c0, width) in zip(o_refs, windows):
        o_ref[:, :width] = x[:, c0:c0 + width].astype(o_ref.dtype)
        if o_ref.shape[1] > width:
            o_ref[:, width:] = jnp.zeros((o_ref.shape[0], o_ref.shape[1] - width), o_ref.dtype)
        if pad_rows:
            @pl.when(i >= n_row_blocks)
            def _():
                o_ref[...] = jnp.zeros_like(o_ref)


def _cast_bf16(w, windows, *, rows_out=None, tr=256):
    rows, cols = w.shape
    rows_out = rows if rows_out is None else rows_out
    tr = min(tr, rows)
    assert rows % tr == 0 and rows_out % tr == 0
    n_in = rows // tr
    blk = _nbytes((tr, cols), F32) + sum(_nbytes((tr, wp), BF16) for _, _, wp in windows)
    return pl.pallas_call(
        functools.partial(_cast_kernel, windows=[(c0, wd) for c0, wd, _ in windows],
                          n_row_blocks=n_in, pad_rows=rows_out > rows),
        grid=(rows_out // tr,),
        in_specs=[pl.BlockSpec((tr, cols), lambda i: (jnp.minimum(i, n_in - 1), 0))],
        out_specs=[pl.BlockSpec((tr, wp), lambda i: (i, 0)) for _, _, wp in windows],
        out_shape=[jax.ShapeDtypeStruct((rows_out, wp), BF16) for _, _, wp in windows],
        compiler_params=pltpu.CompilerParams(
            dimension_semantics=("parallel",), vmem_limit_bytes=_vmem_limit(blk)),
        name="cast_bf16",
    )(w)


def _rms(x, g):
    return x * lax.rsqrt(jnp.mean(x * x, axis=-1, keepdims=True) + RMS_EPS) * g


def _norm_kernel(*refs, has_res, emit_main, second_norm):
    it = iter(refs)
    x_ref, g_ref = next(it), next(it)
    res_ref = next(it) if has_res else None
    g2_ref = next(it) if second_norm else None
    main_ref = next(it) if emit_main else None
    next_ref = next(it) if (second_norm or not emit_main) else None
    y = _rms(x_ref[...].astype(F32), g_ref[...])
    if has_res:
        y = res_ref[...] + y
    if emit_main:
        main_ref[...] = y
    if second_norm:
        next_ref[...] = _rms(y, g2_ref[...]).astype(next_ref.dtype)
    elif not emit_main:
        next_ref[...] = y.astype(next_ref.dtype)


def _norm_call(x, g, res=None, g_next=None, *, emit_main=True, tm=256):
    m, d = x.shape
    tm = min(tm, m)
    second_norm = g_next is not None
    assert emit_main or not second_norm
    row = pl.BlockSpec((tm, d), lambda i: (i, 0))
    vec = pl.BlockSpec((1, d), lambda i: (0, 0))
    args, specs = [x, g.reshape(1, d)], [row, vec]
    if res is not None:
        args.append(res); specs.append(row)
    if second_norm:
        args.append(g_next.reshape(1, d)); specs.append(vec)
    out_shape, out_specs = [], []
    if emit_main:
        out_shape.append(jax.ShapeDtypeStruct((m, d), F32)); out_specs.append(row)
    if second_norm or not emit_main:
        out_shape.append(jax.ShapeDtypeStruct((m, d), BF16)); out_specs.append(row)
    blk = _nbytes((tm, d), F32) * (len(args) + len(out_shape))
    outs = pl.pallas_call(
        functools.partial(_norm_kernel, has_res=res is not None, emit_main=emit_main,
                          second_norm=second_norm),
        grid=(m // tm,),
        in_specs=specs, out_specs=out_specs, out_shape=out_shape,
        compiler_params=pltpu.CompilerParams(
            dimension_semantics=("parallel",), vmem_limit_bytes=_vmem_limit(blk)),
        name="rmsnorm",
    )(*args)
    return outs


def _sigmoid(x):
    return 0.5 * jnp.tanh(0.5 * x) + 0.5


def _mm_kernel(*refs, nk, has_bias, act, has_mul, has_add, w_t):
    it = iter(refs)
    a_ref, w_ref = next(it), next(it)
    up_ref = next(it) if act == "swiglu" else None
    bias_ref = next(it) if has_bias else None
    mul_ref = next(it) if has_mul else None
    add_ref = next(it) if has_add else None
    o_ref = next(it)
    acc_ref = next(it) if nk > 1 else None

    def product():
        if w_t:
            return lax.dot_general(a_ref[...], w_ref[...], (((1,), (1,)), ((), ())),
                                   preferred_element_type=F32)
        return jnp.dot(a_ref[...], w_ref[...], preferred_element_type=F32)

    def finish(y):
        if has_bias:
            y = y + bias_ref[...]
        if act == "sigmoid":
            y = _sigmoid(y)
        elif act == "swiglu":
            up = jnp.dot(a_ref[...], up_ref[...], preferred_element_type=F32)
            y = y * _sigmoid(y) * up
        if has_mul:
            y = y * mul_ref[...].astype(F32)
        if has_add:
            y = y + add_ref[...].astype(F32)
        o_ref[...] = y.astype(o_ref.dtype)

    if nk == 1:
        finish(product())
    else:
        k = pl.program_id(2)

        @pl.when(k == 0)
        def _():
            acc_ref[...] = product()

        @pl.when(jnp.logical_and(k > 0, k < nk - 1))
        def _():
            acc_ref[...] += product()

        @pl.when(k == nk - 1)
        def _():
            finish(acc_ref[...] + product())


def _matmul(a, w, *, out_dtype, tm=1024, tn=1024, tk=None, bias=None, act=None, w_up=None,
            mul=None, mul_col0=0, add=None, w_col0=0, n_out=None, w_t=False):
    m, kdim = a.shape
    n_out = w.shape[0 if w_t else 1] if n_out is None else n_out
    assert not (w_t and w_up is not None)
    tm = min(tm, m)
    tk = kdim if tk is None else tk
    nk = kdim // tk
    assert m % tm == 0 and kdim % tk == 0
    assert n_out % tn == 0 and tn % LANES == 0 and w_col0 % tn == 0
    assert (act == "swiglu") == (w_up is not None) and (w_up is None or nk == 1)
    grid = (m // tm, n_out // tn, nk)
    w_off = w_col0 // tn

    args = [a, w]
    if w_t:
        w_spec = pl.BlockSpec((tn, tk), lambda i, j, k: (j + w_off, k))
    else:
        w_spec = pl.BlockSpec((tk, tn), lambda i, j, k: (k, j + w_off))
    specs = [pl.BlockSpec((tm, tk), lambda i, j, k: (i, k)), w_spec]
    blk = _nbytes((tm, tk), a.dtype) + _nbytes((tk, tn), w.dtype)
    if w_up is not None:
        args.append(w_up)
        specs.append(w_spec)
        blk += _nbytes((tk, tn), w_up.dtype)
    if bias is not None:
        args.append(bias.reshape(1, n_out).astype(F32))
        specs.append(pl.BlockSpec((1, tn), lambda i, j, k: (0, j)))
    if mul is not None:
        assert mul_col0 % tn == 0
        off = mul_col0 // tn
        args.append(mul)
        specs.append(pl.BlockSpec((tm, tn), lambda i, j, k: (i, j + off)))
        blk += _nbytes((tm, tn), mul.dtype)
    if add is not None:
        args.append(add)
        specs.append(pl.BlockSpec((tm, tn), lambda i, j, k: (i, j)))
        blk += _nbytes((tm, tn), add.dtype)
    blk += _nbytes((tm, tn), out_dtype)
    scratch = [pltpu.VMEM((tm, tn), F32)] if nk > 1 else []
    extra = _nbytes((tm, tn), F32) * (4 if (nk > 1 or w_up is not None) else 3)

    return pl.pallas_call(
        functools.partial(_mm_kernel, nk=nk, has_bias=bias is not None, act=act,
                          has_mul=mul is not None, has_add=add is not None, w_t=w_t),
        grid=grid,
        in_specs=specs,
        out_specs=pl.BlockSpec((tm, tn), lambda i, j, k: (i, j)),
        out_shape=jax.ShapeDtypeStruct((m, n_out), out_dtype),
        scratch_shapes=scratch,
        compiler_params=pltpu.CompilerParams(
            dimension_semantics=("parallel", "parallel", "arbitrary"),
            vmem_limit_bytes=int(min(VMEM_CAP, 2 * blk + extra + 4 * 1024 * 1024))),
        name="matmul_" + (act or "plain"),
    )(*args)


def _attn_kernel(sinks_ref, q_ref, kc_ref, kp_ref, vc_ref, vp_ref,
                 cosq_ref, sinq_ref, cosc_ref, sinc_ref, cosp_ref, sinp_ref, o_ref, bias_scr,
                 *, nb, n_kv):
    i = pl.program_id(0)
    first = (i % nb) == 0
    blk = WINDOW

    def rope(t, cos, sin):
        t = t.astype(F32)
        return t * cos + pltpu.roll(t, ATT_HEAD // 2, axis=1) * sin

    cos_q, sin_q = cosq_ref[...], sinq_ref[...]
    cos_c, sin_c = cosc_ref[...], sinc_ref[...]
    cos_p, sin_p = cosp_ref[...], sinp_ref[...]

    qi = lax.broadcasted_iota(jnp.int32, (blk, 2 * blk), 0)
    kj = lax.broadcasted_iota(jnp.int32, (blk, 2 * blk), 1)
    lo = jnp.where(first, blk, 0)
    valid = (kj > qi) & (kj <= qi + blk) & (kj >= lo)
    bias_scr[...] = jnp.where(valid, 0.0, NEG_INF)
    ones = jnp.ones((2 * blk, ATT_HEAD), BF16)

    def scores(h):
        cs = slice(h * ATT_HEAD, (h + 1) * ATT_HEAD)
        k_band = jnp.concatenate(
            [rope(kp_ref[:, cs], cos_p, sin_p), rope(kc_ref[:, cs], cos_c, sin_c)],
            axis=0).astype(BF16)
        q_all = jnp.concatenate(
            [rope(q_ref[:, hq * ATT_HEAD:(hq + 1) * ATT_HEAD], cos_q, sin_q).astype(BF16)
             for hq in range(h * ATT_GROUP, (h + 1) * ATT_GROUP)], axis=0)
        return lax.dot_general(q_all, k_band, (((1,), (1,)), ((), ())),
                               preferred_element_type=F32)

    def finalize(h, pv_parts, sink_parts):
        for g in range(ATT_GROUP):
            hq = h * ATT_GROUP + g
            pv_g = pv_parts[g]
            o_g = pv_g[:, :ATT_HEAD] / (pv_g[:, ATT_HEAD:] + sink_parts[g])
            o_ref[:, hq * ATT_HEAD:(hq + 1) * ATT_HEAD] = o_g.astype(o_ref.dtype)

    s_next, pending = scores(0), None
    for h in range(n_kv):
        s = s_next
        if h + 1 < n_kv:
            s_next = scores(h + 1)
        cs = slice(h * ATT_HEAD, (h + 1) * ATT_HEAD)
        v_ext = jnp.concatenate(
            [jnp.concatenate([vp_ref[:, cs], vc_ref[:, cs]], axis=0), ones], axis=1)
        pv_parts, sink_parts = [], []
        for g in range(ATT_GROUP):
            sink = sinks_ref[h * ATT_GROUP + g]
            s_g = s[g * blk:(g + 1) * blk] + bias_scr[...]
            m_g = jnp.maximum(jnp.max(s_g, axis=-1, keepdims=True), sink)
            p_g = jnp.exp(s_g - m_g).astype(BF16)
            pv_parts.append(jnp.dot(p_g, v_ext, preferred_element_type=F32))
            sink_parts.append(jnp.exp(sink - m_g))
        if pending is not None:
            finalize(*pending)
        pending = (h, pv_parts, sink_parts)
    finalize(*pending)


def _attention(qkv, sinks, cos, sin, *, seq, n_q, n_kv):
    m = qkv.shape[0]
    nb = seq // WINDOW
    q_w, kv_w = n_q * ATT_HEAD, n_kv * ATT_HEAD
    k_blk0, v_blk0 = q_w // kv_w, q_w // kv_w + 1
    prev = lambda i: jnp.maximum(i - 1, 0)
    scale = ATT_HEAD ** -0.5
    tab_c =pl.BlockSpec((WINDOW, ATT_HEAD), lambda i: (i % nb, 0))
    tab_p = pl.BlockSpec((WINDOW, ATT_HEAD), lambda i: (jnp.maximum(i % nb - 1, 0), 0))
    blk = (_nbytes((WINDOW, q_w), BF16) * 2 + 4 * _nbytes((WINDOW, kv_w), BF16)
           + 4 * _nbytes((WINDOW, ATT_HEAD), F32))
    return pl.pallas_call(
        functools.partial(_attn_kernel, nb=nb, n_kv=n_kv),
        grid=(m // WINDOW,),
        in_specs=[
            pl.BlockSpec(memory_space=pltpu.SMEM),
            pl.BlockSpec((WINDOW, q_w), lambda i: (i, 0)),
            pl.BlockSpec((WINDOW, kv_w), lambda i: (i, k_blk0)),
            pl.BlockSpec((WINDOW, kv_w), lambda i: (prev(i), k_blk0)),
            pl.BlockSpec((WINDOW, kv_w), lambda i: (i, v_blk0)),
            pl.BlockSpec((WINDOW, kv_w), lambda i: (prev(i), v_blk0)),
            tab_c, tab_c, tab_c, tab_c, tab_p, tab_p,
        ],
        out_specs=pl.BlockSpec((WINDOW, q_w), lambda i: (i, 0)),
        out_shape=jax.ShapeDtypeStruct((m, q_w), BF16),
        scratch_shapes=[pltpu.VMEM((WINDOW, 2 * WINDOW), F32)],
        compiler_params=pltpu.CompilerParams(
            dimension_semantics=("parallel",), vmem_limit_bytes=_vmem_limit(blk)),
        name="swa_attention",
    )(sinks, qkv, qkv, qkv, qkv, qkv, cos * scale, sin * scale, cos, sin, cos, sin)


def _lora_kernel(x_ref, prev_ref, mu_ref, w0_ref, a0_ref, w2_ref, a2_ref, g2_ref,
                 cum_ref, a_ref, g_ref, *, seq, d_decay, d_aaa):
    i = pl.program_id(0)
    tm = x_ref.shape[0]
    x = x_ref[...]
    last_prev = prev_ref[prev_ref.shape[0] - 1:, :]
    last_prev = jnp.where((i * tm) % seq == 0, 0.0, last_prev)
    row = lax.broadcasted_iota(jnp.int32, x.shape, 0)
    prev = jnp.where(row == 0, last_prev, pltpu.roll(x, 1, axis=0))
    xs = x + (prev - x) * mu_ref[...]
    p_w = jnp.tanh(xs[:, :d_decay]).astype(BF16)
    p_a = xs[:, d_decay:d_decay + d_aaa].astype(BF16)
    p_g = _sigmoid(xs[:, d_decay + d_aaa:]).astype(BF16)
    z = w0_ref[...] + jnp.dot(p_w, w2_ref[...], preferred_element_type=F32)
    log_decay = -DECAY_SCALE * _sigmoid(z)
    shift_c = CHUNK.bit_length() - 1
    tr = lax.broadcasted_iota(jnp.int32, (tm, tm), 0)
    tc = lax.broadcasted_iota(jnp.int32, (tm, tm), 1)
    tri_ones = (((tr >> shift_c) == (tc >> shift_c)) & (tr >= tc)).astype(F32).astype(BF16)
    cum_ref[...] = _split_dot(log_decay, tri_ones, 2, ones_left=True)
    a_ref[...] = _sigmoid(
        a0_ref[...] + jnp.dot(p_a, a2_ref[...], preferred_element_type=F32)).astype(a_ref.dtype)
    g_ref[...] = jnp.dot(p_g, g2_ref[...], preferred_element_type=F32).astype(g_ref.dtype)


def _lora(lora_raw, mu, w0, a0, w2, a2, g2, *, seq, d_decay, d_aaa, tm=256):
    m, wl = lora_raw.shape
    c = w2.shape[1]
    tm = min(tm, seq)
    assert seq % tm == 0 and tm % CHUNK == 0
    sub = SUBLANES
    vec = lambda n: pl.BlockSpec((1, n), lambda i: (0, 0))
    full = lambda a: pl.BlockSpec(a.shape, lambda i: (0, 0))
    out = pl.BlockSpec((tm, c), lambda i: (i, 0))
    blk = (_nbytes((tm, wl), F32) + 3 * _nbytes((tm, c), F32)
           + _nbytes(w2.shape, BF16) + _nbytes(a2.shape, BF16) + _nbytes(g2.shape, BF16))
    return pl.pallas_call(
        functools.partial(_lora_kernel, seq=seq, d_decay=d_decay, d_aaa=d_aaa),
        grid=(m // tm,),
        in_specs=[
            pl.BlockSpec((tm, wl), lambda i: (i, 0)),
            pl.BlockSpec((sub, wl), lambda i: (jnp.maximum(i * (tm // sub) - 1, 0), 0)),
            vec(wl), vec(c), vec(c), full(w2), full(a2), full(g2),
        ],
        out_specs=[out, out, out],
        out_shape=[jax.ShapeDtypeStruct((m, c), F32),
                   jax.ShapeDtypeStruct((m, c), BF16), jax.ShapeDtypeStruct((m, c), BF16)],
        compiler_params=pltpu.CompilerParams(
            dimension_semantics=("parallel",), vmem_limit_bytes=_vmem_limit(blk)),
        name="rwkv_lora",
    )(lora_raw, lora_raw, mu.reshape(1, wl), w0.reshape(1, c), a0.reshape(1, c), w2, a2, g2)


def _bf(x):
    return x.astype(BF16)


def _dot(a, b):
    return jnp.dot(_bf(a), _bf(b), preferred_element_type=F32)


def _dot_nt(a, b):
    return lax.dot_general(_bf(a), _bf(b), (((1,), (1,)), ((), ())), preferred_element_type=F32)


def _dot_tn(a, b):
    return lax.dot_general(_bf(a), _bf(b), (((0,), (0,)), ((), ())), preferred_element_type=F32)


def _split_dot(x, ones_bf, parts, ones_left=False):
    acc, rem = None, x
    for p in range(parts):
        hi = _bf(rem)
        ops = (ones_bf, hi) if ones_left else (hi, ones_bf)
        term = jnp.dot(*ops, preferred_element_type=F32)
        acc = term if acc is None else acc + term
        if p + 1 < parts:
            rem = rem - hi.astype(F32)
    return acc


def _wkv_kernel(r_ref, k_ref, v_ref, cum_ref, a_ref, g_ref,
                mur_ref, muk_ref, muv_ref, kk_ref, ka_ref, rk_ref, lnw_ref, lnb_ref,
                o_ref, rm_scr, yg_scr, bonus_scr, y_scr, h_scr, *, n_chunks, group):
    c_len, two = CHUNK, 2 * CHUNK
    g_rows = group * c_len
    shift_c = CHUNK.bit_length() - 1
    lane = lax.broadcasted_iota(jnp.int32, (c_len, PAIR), 1)
    m0 = (lane < RWKV_HEAD).astype(F32)
    m1 = 1.0 - m0
    row = lax.broadcasted_iota(jnp.int32, (two, two), 0)
    col = lax.broadcasted_iota(jnp.int32, (two, two), 1)
    same = (row >> shift_c) == (col >> shift_c)
    strict = same & ((row & (CHUNK - 1)) > (col & (CHUNK - 1)))
    incl = same & ((row & (CHUNK - 1)) >= (col & (CHUNK - 1)))
    eye = (row == col).astype(F32)
    head_ones = _bf(same.astype(F32))
    g_row = lax.broadcasted_iota(jnp.int32, (g_rows, PAIR), 0)
    row0 = g_row == 0
    chunk_row0 = (g_row & (CHUNK - 1)) == 0

    mu_r, mu_k, mu_v = mur_ref[...], muk_ref[...], muv_ref[...]
    k_k, k_a, r_k = kk_ref[...], ka_ref[...], rk_ref[...]

    def stack(x):
        return jnp.concatenate([x * m0, x * m1], axis=0)

    rm_scr[pl.ds(0, group * 2 * two), :] = jnp.zeros((group * 2 * two, PAIR), BF16)
    yg_scr[pl.ds(0, group * 2 * two), :] = jnp.zeros((group * 2 * two, PAIR), F32)
    h_scr[...] = jnp.zeros_like(h_scr)
    y_scr[pl.ds(0, g_rows), :] = jnp.zeros((g_rows, PAIR), F32)

    def serial_step(slot, h):
        base = pl.multiple_of(slot * (2 * two), 2 * two)
        out = jnp.dot(rm_scr[pl.ds(base, 2 * two), :], h,
                      preferred_element_type=F32) + yg_scr[pl.ds(base, 2 * two), :]
        rows = pl.ds(pl.multiple_of((slot + group) * c_len, c_len), c_len)
        y_scr[rows, :] = out[:c_len] + out[c_len:two]
        return _bf(out[two:])

    n_boundaries = 9
    serial_at = [j * n_boundaries // group for j in range(group)]
    n_groups = n_chunks // group
    ln_w, ln_b = lnw_ref[...], lnb_ref[...]
    inv_n = 1.0 / RWKV_HEAD

    def group_rows(gi):
        start = gi * g_rows
        return pl.ds(start if isinstance(gi, int) else pl.multiple_of(start, g_rows), g_rows)

    def finish(y_blocks, groups):
        rows = [group_rows(gi) for gi in groups]
        ys = [y_scr[group_rows(yb), :] for yb in y_blocks]
        ds = [y - _split_dot(y, head_ones, 1) * inv_n for y in ys]
        vs = [_split_dot(d * d, head_ones, 1) * inv_n for d in ds]
        return [((d * lax.rsqrt(var + GN_EPS) * ln_w + ln_b + bonus_scr[r, :])
                 * g_ref[r, :].astype(F32)).astype(o_ref.dtype) for r, d, var in zip(rows, ds, vs)]

    def local(i, _):
        rows = pl.ds(pl.multiple_of(i * g_rows, g_rows), g_rows)
        before = pl.ds(pl.multiple_of(jnp.maximum(i * g_rows - SUBLANES, 0), SUBLANES), SUBLANES)
        keep_prev = jnp.where(i > 0, 1.0, 0.0)
        state = [h_scr[...]]

        def boundary(k):
            for j in range(group):
                if serial_at[j] == k:
                    state[0] = serial_step(i * group + j, state[0])

        def shift(ref, mu):
            x = ref[rows, :]
            last = ref[before, :][SUBLANES - 1:, :] * keep_prev
            prev = jnp.where(row0, last, pltpu.roll(x, 1, axis=0))
            return x + (prev - x) * mu

        rs, ks, vs = shift(r_ref, mu_r), shift(k_ref, mu_k), shift(v_ref, mu_v)
        cum, a = cum_ref[rows, :], a_ref[rows, :].astype(F32)

        kk = ks * k_k
        norm = jnp.sqrt(_split_dot(kk * kk, head_ones, 1))
        kk = kk / jnp.maximum(norm, 1e-12)
        kx = ks * (1.0 + (a - 1.0) * k_a)
        b = kk * a
        bonus_scr[rows, :] = _split_dot(rs * kx * r_k, head_ones, 1) * vs
        (o_block,) = finish([i], [jnp.maximum(i - 2, 0)])

        cum_before = jnp.where(chunk_row0, 0.0, pltpu.roll(cum, 1, axis=0))
        e_neg = jnp.exp(-cum)
        at_g = -kk * jnp.exp(cum_before)
        rt_g = rs * jnp.exp(cum)
        bt_g, kt_g = _bf(b * e_neg), _bf(kx * e_neg)

        st = []
        for j in range(group):
            sl = slice(j * c_len, (j + 1) * c_len)
            c_end = cum[(j + 1) * c_len - 1:(j + 1) * c_len, :]
            e_rem = jnp.exp(c_end - cum[sl])
            at2, rt2 = stack(at_g[sl]), stack(rt_g[sl])
            st.append(dict(
                at2=at2, rt2=rt2, v2=_bf(stack(vs[sl])), d_end=jnp.exp(c_end),
                bk2=jnp.concatenate([stack(b[sl] * e_rem), stack(kx[sl] * e_rem)], axis=0),
                s=_dot_nt(jnp.concatenate([at2, rt2], axis=0),
                          jnp.concatenate([bt_g[sl], bt_g[sl], kt_g[sl], kt_g[sl]], axis=0))))
        boundary(0)
        for d in st:
            s = d.pop("s")
            x = jnp.where(strict, s[:two, :two], 0.0)
            a_ak = jnp.where(strict, s[:two, two:], 0.0)
            d["rbk"] = jnp.concatenate([jnp.where(incl, s[two:, :two], 0.0),
                                        jnp.where(incl, s[two:, two:], 0.0)], axis=1)
            d["t_sum"] = eye + x
            d["x_pow"] = _dot(x, x)
            d["z2"] = _dot(a_ak, d["v2"])
        boundary(1)
        for step in range(CHUNK.bit_length() - 3):
            for d in st:
                both = _dot(d["x_pow"], jnp.concatenate([d["t_sum"], d["x_pow"]], axis=1))
                d["t_sum"] = d["t_sum"] + both[:, :two]
                d["x_pow"] = both[:, two:]
            boundary(2 + step)
        for d in st:
            d["t_inv"] = d["t_sum"] + _dot(d["x_pow"], d["t_sum"])
        boundary(6)
        for d in st:
            pq = _dot(d["t_inv"], jnp.concatenate([_bf(d["at2"]), _bf(d["z2"])], axis=1))
            d["rhs"] = jnp.concatenate(
                [_bf(pq), jnp.concatenate([jnp.zeros_like(d["v2"]), d["v2"]], axis=1)], axis=0)
        boundary(7)
        for d in st:
            d["n1"] = _dot(d["rbk"], d["rhs"])
            d["n2"] = _dot_tn(d["bk2"], d["rhs"])
        boundary(8)
        for j, d in enumerate(st):
            n1, n2 = d["n1"], d["n2"]
            rp2 = d["rt2"] + n1[:, :two]
            m_blk = n2[:, :two] + eye * d["d_end"]
            base = pl.multiple_of(((i + 1) * group + j) * (2 * two), 2 * two)
            rm_scr[pl.ds(base, 2 * two), :] = _bf(jnp.concatenate([rp2, m_blk], axis=0))
            yg_scr[pl.ds(base, 2 * two), :] = jnp.concatenate([n1[:, two:], n2[:, two:]], axis=0)
        h_scr[...] = state[0]

        @pl.when(i >= 2)
        def _():
            o_ref[group_rows(jnp.maximum(i - 2, 0)), :] = o_block

        return 0

    lax.fori_loop(0, n_groups, local, 0)

    def drain(j, _):
        h_scr[...] = serial_step(n_chunks + j, h_scr[...])
        return 0

    lax.fori_loop(0, group, drain, 0)

    tail = list(range(max(n_groups - 2, 0), n_groups))
    for gi, o_block in zip(tail, finish([gi + 2 for gi in tail], tail)):
        o_ref[group_rows(gi), :] = o_block


def _wkv(rkv, cum, a, g, mu_rkv, k_k, k_a, r_k, ln_w, ln_b, *, batch, seq, group=8):
    m, c = cum.shape
    n_pairs = c // PAIR
    n_chunks = seq // CHUNK
    assert seq % (group * CHUNK) == 0
    col = lambda off: pl.BlockSpec((seq, PAIR), lambda b, p: (b, p + off))
    vec = lambda off: pl.BlockSpec((1, PAIR), lambda b, p: (0, p + off))
    vecs = [x.reshape(1, c) for x in (k_k, k_a, r_k, ln_w, ln_b)]
    blk = 6 * _nbytes((seq, PAIR), F32) + _nbytes((seq, PAIR), BF16)
    slot_rows = (n_chunks + group) * 4 * CHUNK
    y_rows = seq + 2 * group * CHUNK
    scratch = [pltpu.VMEM((slot_rows, PAIR), BF16),
               pltpu.VMEM((slot_rows, PAIR), F32),
               pltpu.VMEM((seq, PAIR), F32),
               pltpu.VMEM((y_rows, PAIR), F32),
               pltpu.VMEM((PAIR, PAIR), BF16)]
    scr = (_nbytes((slot_rows, PAIR), BF16) + _nbytes((slot_rows, PAIR), F32)
           + _nbytes((seq, PAIR), F32) + _nbytes((y_rows, PAIR), F32))
    return pl.pallas_call(
        functools.partial(_wkv_kernel, n_chunks=n_chunks, group=group),
        grid=(batch, n_pairs),
        in_specs=[col(0), col(n_pairs), col(2 * n_pairs), col(0), col(0), col(0),
                  vec(0), vec(n_pairs), vec(2 * n_pairs)] + [vec(0)] * 5,
        out_specs=col(0),
        out_shape=jax.ShapeDtypeStruct((m, c), BF16),
        scratch_shapes=scratch,
        compiler_params=pltpu.CompilerParams(
            dimension_semantics=("parallel", "parallel"),
            vmem_limit_bytes=int(min(VMEM_CAP, 2 * blk + scr + 8 * 1024 * 1024))),
        name="wkv7_chunked",
    )(rkv, rkv, rkv, cum, a, g, mu_rkv, mu_rkv, mu_rkv, *vecs)


def _rope_tables(seq):
    pos = jnp.arange(seq, dtype=F32)
    inv_freq = ROPE_THETA ** (-jnp.arange(0, ATT_HEAD, 2, dtype=F32) / ATT_HEAD)
    ang = pos[:, None] * inv_freq[None, :]
    cos, sin = jnp.cos(ang), jnp.sin(ang)
    return jnp.concatenate([cos, cos], axis=-1), jnp.concatenate([-sin, sin], axis=-1)


def _pad_to(x, n, axis):
    pad = n - x.shape[axis]
    if pad == 0:
        return x
    widths = [(0, 0)] * x.ndim
    widths[axis] = (0, pad)
    return jnp.pad(x, widths)


def _round_up(n, mult):
    return -(-n // mult) * mult


def _layer(x2, batch, seq, norm_mix_pre, norm_mix_post, norm_ffn_pre, norm_ffn_post, w_in, b_qkv,
           att_sinks, mu_shift, w0, w2, a0, a2, g2, k_k, k_a, r_k, ln_x_w, ln_x_b,
           w_att_branch, w_rwkv_branch, w_out, w_ffn_gate, w_ffn_up, w_ffn_down):
    d = x2.shape[1]
    n_q = att_sinks.shape[0]
    q_w = n_q * ATT_HEAD
    qkv_w = b_qkv.shape[0]
    n_kv = (qkv_w - q_w) // (2 * ATT_HEAD)
    c = w0.shape[0]
    d_decay, d_aaa, d_gate = w2.shape[0], a2.shape[0], g2.shape[0]
    lora_w = d_decay + d_aaa + d_gate
    lora_pad = _round_up(lora_w, LANES)
    ffn = w_ffn_gate.shape[1]

    c0 = qkv_w
    lora_col0 = c0 + 3 * c
    in_dim = w_in.shape[1]
    row_tile = max(t for t in range(16, 513, 16) if in_dim % t == 0)
    (w_in_t,) = _cast_bf16(w_in.T, [(0, d, d)], tr=row_tile)
    w_gates_t = w_in_t[lora_col0 + lora_w:]
    mu_rkv = mu_shift[:3 * c].reshape(1, 3 * c)
    mu_lora = _pad_to(mu_shift[3 * c:], lora_pad, 0)
    g2_pad = _pad_to(_bf(g2), lora_pad - d_decay - d_aaa, 0)

    (h,) = _norm_call(x2, norm_mix_pre, emit_main=False)

    qkv = _matmul(h, w_in_t, w_t=True, n_out=c0, out_dtype=BF16, bias=b_qkv)
    rkv = _matmul(h, w_in_t, w_t=True, w_col0=c0, n_out=3 * c, out_dtype=F32)
    lora_raw = _matmul(h, w_in_t, w_t=True, w_col0=lora_col0, n_out=lora_pad, out_dtype=F32,
                       tn=lora_pad)
    gates = _matmul(h, w_gates_t, w_t=True, out_dtype=BF16, act="sigmoid")

    cos, sin = _rope_tables(seq)
    o_att = _attention(qkv, att_sinks.astype(F32), cos, sin, seq=seq, n_q=n_q, n_kv=n_kv)

    cum, a_gate, g_out = _lora(lora_raw, mu_lora, w0, a0, _bf(w2), _bf(a2), g2_pad,
                              seq=seq, d_decay=d_decay, d_aaa=d_aaa)
    o_rwkv = _wkv(rkv, cum, a_gate, g_out, mu_rkv, k_k, k_a, r_k.reshape(-1), ln_x_w, ln_x_b,
                  batch=batch, seq=seq)

    m_att = _matmul(o_att, _bf(w_att_branch), out_dtype=BF16, mul=gates, mul_col0=0)
    merged = _matmul(o_rwkv, _bf(w_rwkv_branch), out_dtype=BF16, mul=gates, mul_col0=d,
                     add=m_att)
    mixed = _matmul(merged, _bf(w_out), out_dtype=BF16)
    x1, h2 = _norm_call(mixed, norm_mix_post, res=x2, g_next=norm_ffn_pre)

    tn_ffn = 512
    ffn_pad = _round_up(ffn, 2 * tn_ffn)
    (w_gate,) = _cast_bf16(w_ffn_gate, [(0, ffn, ffn_pad)], tr=128)
    (w_up,) = _cast_bf16(w_ffn_up, [(0, ffn, ffn_pad)], tr=128)
    (w_dn,) = _cast_bf16(w_ffn_down, [(0, d, d)], rows_out=ffn_pad)
    act = _matmul(h2, w_gate, w_up=w_up, out_dtype=BF16, act="swiglu", tn=tn_ffn)
    f = _matmul(act, w_dn, out_dtype=BF16, tk=ffn_pad // 4)
    (out,) = _norm_call(f, norm_ffn_post, res=x1)
    return out


def kernel(x, norm_mix_pre, norm_mix_post, norm_ffn_pre, norm_ffn_post, w_in, b_qkv, att_sinks,
           mu_shift, w0, w2, a0, a2, g2, k_k, k_a, r_k, ln_x_w, ln_x_b,
           w_att_branch, w_rwkv_branch, w_out, w_ffn_gate, w_ffn_up, w_ffn_down):
    batch, seq, d = x.shape
    x2 = x.reshape(batch * seq, d)
    for l in range(w_in.shape[0]):
        x2 = _layer(x2, batch, seq, norm_mix_pre[l], norm_mix_post[l], norm_ffn_pre[l],
                    norm_ffn_post[l], w_in[l], b_qkv[l], att_sinks[l], mu_shift[l], w0[l], w2[l],
                    a0[l], a2[l], g2[l], k_k[l], k_a[l], r_k[l], ln_x_w[l], ln_x_b[l],
                    w_att_branch[l], w_rwkv_branch[l], w_out[l],
                    w_ffn_gate[l], w_ffn_up[l], w_ffn_down[l])
    return x2.reshape(batch, seq, d)
```

```python
import functools
import math

import jax
import jax.numpy as jnp
from jax import lax
from jax.experimental import pallas as pl
from jax.experimental.pallas import tpu as pltpu

F32 = jnp.float32
BF16 = jnp.bfloat16

LANES = 128
SUBLANES = 8
V7X_VMEM_BYTES = 64 * 1024 * 1024
VMEM_CAP = V7X_VMEM_BYTES - 6 * 1024 * 1024

ATT_HEAD = 128
ATT_GROUP = 4
WINDOW = 128
ROPE_THETA = 10000.0
NEG_INF = -1e30
RWKV_HEAD = 64
CHUNK = 64
PAIR = 2 * RWKV_HEAD
GN_EPS = 64e-5
RMS_EPS = 1e-6
DECAY_SCALE = math.exp(-0.5)


def _vmem_limit(block_bytes):
    return int(min(VMEM_CAP, 2 * block_bytes + 16 * 1024 * 1024))


def _nbytes(shape, dtype):
    n = 1
    for s in shape:
        n *= s
    return n * jnp.dtype(dtype).itemsize


def _cast_kernel(x_ref, *o_refs, windows, n_row_blocks, pad_rows):
    i = pl.program_id(0)
    x = x_ref[...]
    for o_ref, (c0, width) in zip(o_refs, windows):
        o_ref[:, :width] = x[:, c0:c0 + width].astype(o_ref.dtype)
        if o_ref.shape[1] > width:
            o_ref[:, width:] = jnp.zeros((o_ref.shape[0], o_ref.shape[1] - width), o_ref.dtype)
        if pad_rows:
            @pl.when(i >= n_row_blocks)
            def _():
                o_ref[...] = jnp.zeros_like(o_ref)


def _cast_bf16(w, windows, *, rows_out=None, tr=256):
    rows, cols = w.shape
    rows_out = rows if rows_out is None else rows_out
    tr = min(tr, rows)
    assert rows % tr == 0 and rows_out % tr == 0
    n_in = rows // tr
    blk = _nbytes((tr, cols), F32) + sum(_nbytes((tr, wp), BF16) for _, _, wp in windows)
    return pl.pallas_call(
        functools.partial(_cast_kernel, windows=[(c0, wd) for c0, wd, _ in windows],
                          n_row_blocks=n_in, pad_rows=rows_out > rows),
        grid=(rows_out // tr,),
        in_specs=[pl.BlockSpec((tr, cols), lambda i: (jnp.minimum(i, n_in - 1), 0))],
        out_specs=[pl.BlockSpec((tr, wp), lambda i: (i, 0)) for _, _, wp in windows],
        out_shape=[jax.ShapeDtypeStruct((rows_out, wp), BF16) for _, _, wp in windows],
        compiler_params=pltpu.CompilerParams(
            dimension_semantics=("parallel",), vmem_limit_bytes=_vmem_limit(blk)),
        name="cast_bf16",
    )(w)


def _rms(x, g):
    return x * lax.rsqrt(jnp.mean(x * x, axis=-1, keepdims=True) + RMS_EPS) * g


def _norm_kernel(*refs, has_res, emit_main, second_norm):
    it = iter(refs)
    x_ref, g_ref = next(it), next(it)
    res_ref = next(it) if has_res else None
    g2_ref = next(it) if second_norm else None
    main_ref = next(it) if emit_main else None
    next_ref = next(it) if (second_norm or not emit_main) else None
    y = _rms(x_ref[...].astype(F32), g_ref[...])
    if has_res:
        y = res_ref[...] + y
    if emit_main:
        main_ref[...] = y
    if second_norm:
        next_ref[...] = _rms(y, g2_ref[...]).astype(next_ref.dtype)
    elif not emit_main:
        next_ref[...] = y.astype(next_ref.dtype)


def _norm_call(x, g, res=None, g_next=None, *, emit_main=True, tm=256):
    m, d = x.shape
    tm = min(tm, m)
    second_norm = g_next is not None
    assert emit_main or not second_norm
    row = pl.BlockSpec((tm, d), lambda i: (i, 0))
    vec = pl.BlockSpec((1, d), lambda i: (0, 0))
    args, specs = [x, g.reshape(1, d)], [row, vec]
    if res is not None:
        args.append(res); specs.append(row)
    if second_norm:
        args.append(g_next.reshape(1, d)); specs.append(vec)
    out_shape, out_specs = [], []
    if emit_main:
        out_shape.append(jax.ShapeDtypeStruct((m, d), F32)); out_specs.append(row)
    if second_norm or not emit_main:
        out_shape.append(jax.ShapeDtypeStruct((m, d), BF16)); out_specs.append(row)
    blk = _nbytes((tm, d), F32) * (len(args) + len(out_shape))
    outs = pl.pallas_call(
        functools.partial(_norm_kernel, has_res=res is not None, emit_main=emit_main,
                          second_norm=second_norm),
        grid=(m // tm,),
        in_specs=specs, out_specs=out_specs, out_shape=out_shape,
        compiler_params=pltpu.CompilerParams(
            dimension_semantics=("parallel",), vmem_limit_bytes=_vmem_limit(blk)),
        name="rmsnorm",
    )(*args)
    return outs


def _sigmoid(x):
    return 0.5 * jnp.tanh(0.5 * x) + 0.5


def _mm_kernel(*refs, nk, has_bias, act, has_mul, has_add, w_t):
    it = iter(refs)
    a_ref, w_ref = next(it), next(it)
    up_ref = next(it) if act == "swiglu" else None
    bias_ref = next(it) if has_bias else None
    mul_ref = next(it) if has_mul else None
    add_ref = next(it) if has_add else None
    o_ref = next(it)
    acc_ref = next(it) if nk > 1 else None

    def product():
        if w_t:
            return lax.dot_general(a_ref[...], w_ref[...], (((1,), (1,)), ((), ())),
                                   preferred_element_type=F32)
        return jnp.dot(a_ref[...], w_ref[...], preferred_element_type=F32)

    def finish(y):
        if has_bias:
            y = y + bias_ref[...]
        if act == "sigmoid":
            y = _sigmoid(y)
        elif act == "swiglu":
            up = jnp.dot(a_ref[...], up_ref[...], preferred_element_type=F32)
            y = y * _sigmoid(y) * up
        if has_mul:
            y = y * mul_ref[...].astype(F32)
        if has_add:
            y = y + add_ref[...].astype(F32)
        o_ref[...] = y.astype(o_ref.dtype)

    if nk == 1:
        finish(product())
    else:
        k = pl.program_id(2)

        @pl.when(k == 0)
        def _():
            acc_ref[...] = product()

        @pl.when(jnp.logical_and(k > 0, k < nk - 1))
        def _():
            acc_ref[...] += product()

        @pl.when(k == nk - 1)
        def _():
            finish(acc_ref[...] + product())


def _matmul(a, w, *, out_dtype, tm=1024, tn=1024, tk=None, bias=None, act=None, w_up=None,
            mul=None, mul_col0=0, add=None, w_col0=0, n_out=None, w_t=False):
    m, kdim = a.shape
    n_out = w.shape[0 if w_t else 1] if n_out is None else n_out
    assert not (w_t and w_up is not None)
    tm = min(tm, m)
    tk = kdim if tk is None else tk
    nk = kdim // tk
    assert m % tm == 0 and kdim % tk == 0
    assert n_out % tn == 0 and tn % LANES == 0 and w_col0 % tn == 0
    assert (act == "swiglu") == (w_up is not None) and (w_up is None or nk == 1)
    grid = (m // tm, n_out // tn, nk)
    w_off = w_col0 // tn

    args = [a, w]
    if w_t:
        w_spec = pl.BlockSpec((tn, tk), lambda i, j, k: (j + w_off, k))
    else:
        w_spec = pl.BlockSpec((tk, tn), lambda i, j, k: (k, j + w_off))
    specs = [pl.BlockSpec((tm, tk), lambda i, j, k: (i, k)), w_spec]
    blk = _nbytes((tm, tk), a.dtype) + _nbytes((tk, tn), w.dtype)
    if w_up is not None:
        args.append(w_up)
        specs.append(w_spec)
        blk += _nbytes((tk, tn), w_up.dtype)
    if bias is not None:
        args.append(bias.reshape(1, n_out).astype(F32))
        specs.append(pl.BlockSpec((1, tn), lambda i, j, k: (0, j)))
    if mul is not None:
        assert mul_col0 % tn == 0
        off = mul_col0 // tn
        args.append(mul)
        specs.append(pl.BlockSpec((tm, tn), lambda i, j, k: (i, j + off)))
        blk += _nbytes((tm, tn), mul.dtype)
    if add is not None:
        args.append(add)
        specs.append(pl.BlockSpec((tm, tn), lambda i, j, k: (i, j)))
        blk += _nbytes((tm, tn), add.dtype)
    blk += _nbytes((tm, tn), out_dtype)
    scratch = [pltpu.VMEM((tm, tn), F32)] if nk > 1 else []
    extra = _nbytes((tm, tn), F32) * (4 if (nk > 1 or w_up is not None) else 3)

    return pl.pallas_call(
        functools.partial(_mm_kernel, nk=nk, has_bias=bias is not None, act=act,
                          has_mul=mul is not None, has_add=add is not None, w_t=w_t),
        grid=grid,
        in_specs=specs,
        out_specs=pl.BlockSpec((tm, tn), lambda i, j, k: (i, j)),
        out_shape=jax.ShapeDtypeStruct((m, n_out), out_dtype),
        scratch_shapes=scratch,
        compiler_params=pltpu.CompilerParams(
            dimension_semantics=("parallel", "parallel", "arbitrary"),
            vmem_limit_bytes=int(min(VMEM_CAP, 2 * blk + extra + 4 * 1024 * 1024))),
        name="matmul_" + (act or "plain"),
    )(*args)


def _attn_kernel(sinks_ref, q_ref, kc_ref, kp_ref, vc_ref, vp_ref,
                 cosq_ref, sinq_ref, cosc_ref, sinc_ref, cosp_ref, sinp_ref, o_ref, bias_scr,
                 *, nb, n_kv):
    i = pl.program_id(0)
    first = (i % nb) == 0
    blk = WINDOW

    def rope(t, cos, sin):
        t = t.astype(F32)
        return t * cos + pltpu.roll(t, ATT_HEAD // 2, axis=1) * sin

    cos_q, sin_q = cosq_ref[...], sinq_ref[...]
    cos_c, sin_c = cosc_ref[...], sinc_ref[...]
    cos_p, sin_p = cosp_ref[...], sinp_ref[...]

    qi = lax.broadcasted_iota(jnp.int32, (blk, 2 * blk), 0)
    kj = lax.broadcasted_iota(jnp.int32, (blk, 2 * blk), 1)
    lo = jnp.where(first, blk, 0)
    valid = (kj > qi) & (kj <= qi + blk) & (kj >= lo)
    bias_scr[...] = jnp.where(valid, 0.0, NEG_INF)
    ones = jnp.ones((2 * blk, ATT_HEAD), BF16)

    def scores(h):
        cs = slice(h * ATT_HEAD, (h + 1) * ATT_HEAD)
        k_band = jnp.concatenate(
            [rope(kp_ref[:, cs], cos_p, sin_p), rope(kc_ref[:, cs], cos_c, sin_c)],
            axis=0).astype(BF16)
        q_all = jnp.concatenate(
            [rope(q_ref[:, hq * ATT_HEAD:(hq + 1) * ATT_HEAD], cos_q, sin_q).astype(BF16)
             for hq in range(h * ATT_GROUP, (h + 1) * ATT_GROUP)], axis=0)
        return lax.dot_general(q_all, k_band, (((1,), (1,)), ((), ())),
                               preferred_element_type=F32)

    def finalize(h, pv, sink_parts):
        for g in range(ATT_GROUP):
            hq = h * ATT_GROUP + g
            pv_g = pv[g * blk:(g + 1) * blk]
            o_g = pv_g[:, :ATT_HEAD] / (pv_g[:, ATT_HEAD:] + sink_parts[g])
            o_ref[:, hq * ATT_HEAD:(hq + 1) * ATT_HEAD] = o_g.astype(o_ref.dtype)

    s_next, pending = scores(0), None
    for h in range(n_kv):
        s = s_next
        if h + 1 < n_kv:
            s_next = scores(h + 1)
        cs = slice(h * ATT_HEAD, (h + 1) * ATT_HEAD)
        v_ext = jnp.concatenate(
            [jnp.concatenate([vp_ref[:, cs], vc_ref[:, cs]], axis=0), ones], axis=1)
        p_parts, sink_parts = [], []
        for g in range(ATT_GROUP):
            sink = sinks_ref[h * ATT_GROUP + g]
            s_g = s[g * blk:(g + 1) * blk] + bias_scr[...]
            m_g = jnp.maximum(jnp.max(s_g, axis=-1, keepdims=True), sink)
            p_parts.append(jnp.exp(s_g - m_g).astype(BF16))
            sink_parts.append(jnp.exp(sink - m_g))
        pv = jnp.dot(jnp.concatenate(p_parts, axis=0), v_ext,
                     preferred_element_type=F32)
        if pending is not None:
            finalize(*pending)
        pending = (h, pv, sink_parts)
    finalize(*pending)


def _attention(qkv, sinks, cos, sin, *, seq, n_q, n_kv):
    m = qkv.shape[0]
    nb = seq // WINDOW
    q_w, kv_w = n_q * ATT_HEAD, n_kv * ATT_HEAD
    k_blk0, v_blk0 = q_w // kv_w, q_w // kv_w + 1
    prev = lambda i: jnp.maximum(i - 1, 0)
    scale = ATT_HEAD ** -0.5
    tab_c =pl.BlockSpec((WINDOW, ATT_HEAD), lambda i: (i % nb, 0))
    tab_p = pl.BlockSpec((WINDOW, ATT_HEAD), lambda i: (jnp.maximum(i % nb - 1, 0), 0))
    blk = (_nbytes((WINDOW, q_w), BF16) * 2 + 4 * _nbytes((WINDOW, kv_w), BF16)
           + 4 * _nbytes((WINDOW, ATT_HEAD), F32))
    return pl.pallas_call(
        functools.partial(_attn_kernel, nb=nb, n_kv=n_kv),
        grid=(m // WINDOW,),
        in_specs=[
            pl.BlockSpec(memory_space=pltpu.SMEM),
            pl.BlockSpec((WINDOW, q_w), lambda i: (i, 0)),
            pl.BlockSpec((WINDOW, kv_w), lambda i: (i, k_blk0)),
            pl.BlockSpec((WINDOW, kv_w), lambda i: (prev(i), k_blk0)),
            pl.BlockSpec((WINDOW, kv_w), lambda i: (i, v_blk0)),
            pl.BlockSpec((WINDOW, kv_w), lambda i: (prev(i), v_blk0)),
            tab_c, tab_c, tab_c, tab_c, tab_p, tab_p,
        ],
        out_specs=pl.BlockSpec((WINDOW, q_w), lambda i: (i, 0)),
        out_shape=jax.ShapeDtypeStruct((m, q_w), BF16),
        scratch_shapes=[pltpu.VMEM((WINDOW, 2 * WINDOW), F32)],
        compiler_params=pltpu.CompilerParams(
            dimension_semantics=("parallel",), vmem_limit_bytes=_vmem_limit(blk)),
        name="swa_attention",
    )(sinks, qkv, qkv, qkv, qkv, qkv, cos * scale, sin * scale, cos, sin, cos, sin)


def _lora_kernel(x_ref, prev_ref, mu_ref, w0_ref, a0_ref, w2_ref, a2_ref, g2_ref,
                 cum_ref, a_ref, g_ref, *, seq, d_decay, d_aaa):
    i = pl.program_id(0)
    tm = x_ref.shape[0]
    x = x_ref[...]
    last_prev = prev_ref[prev_ref.shape[0] - 1:, :]
    last_prev = jnp.where((i * tm) % seq == 0, 0.0, last_prev)
    row = lax.broadcasted_iota(jnp.int32, x.shape, 0)
    prev = jnp.where(row == 0, last_prev, pltpu.roll(x, 1, axis=0))
    xs = x + (prev - x) * mu_ref[...]
    p_w = jnp.tanh(xs[:, :d_decay]).astype(BF16)
    p_a = xs[:, d_decay:d_decay + d_aaa].astype(BF16)
    p_g = _sigmoid(xs[:, d_decay + d_aaa:]).astype(BF16)
    z = w0_ref[...] + jnp.dot(p_w, w2_ref[...], preferred_element_type=F32)
    log_decay = -DECAY_SCALE * _sigmoid(z)
    shift_c = CHUNK.bit_length() - 1
    tr = lax.broadcasted_iota(jnp.int32, (tm, tm), 0)
    tc = lax.broadcasted_iota(jnp.int32, (tm, tm), 1)
    tri_ones = (((tr >> shift_c) == (tc >> shift_c)) & (tr >= tc)).astype(F32).astype(BF16)
    cum_ref[...] = _split_dot(log_decay, tri_ones, 2, ones_left=True)
    a_ref[...] = _sigmoid(
        a0_ref[...] + jnp.dot(p_a, a2_ref[...], preferred_element_type=F32)).astype(a_ref.dtype)
    g_ref[...] = jnp.dot(p_g, g2_ref[...], preferred_element_type=F32).astype(g_ref.dtype)


def _lora(lora_raw, mu, w0, a0, w2, a2, g2, *, seq, d_decay, d_aaa, tm=256):
    m, wl = lora_raw.shape
    c = w2.shape[1]
    tm = min(tm, seq)
    assert seq % tm == 0 and tm % CHUNK == 0
    sub = SUBLANES
    vec = lambda n: pl.BlockSpec((1, n), lambda i: (0, 0))
    full = lambda a: pl.BlockSpec(a.shape, lambda i: (0, 0))
    out = pl.BlockSpec((tm, c), lambda i: (i, 0))
    blk = (_nbytes((tm, wl), F32) + 3 * _nbytes((tm, c), F32)
           + _nbytes(w2.shape, BF16) + _nbytes(a2.shape, BF16) + _nbytes(g2.shape, BF16))
    return pl.pallas_call(
        functools.partial(_lora_kernel, seq=seq, d_decay=d_decay, d_aaa=d_aaa),
        grid=(m // tm,),
        in_specs=[
            pl.BlockSpec((tm, wl), lambda i: (i, 0)),
            pl.BlockSpec((sub, wl), lambda i: (jnp.maximum(i * (tm // sub) - 1, 0), 0)),
            vec(wl), vec(c), vec(c), full(w2), full(a2), full(g2),
        ],
        out_specs=[out, out, out],
        out_shape=[jax.ShapeDtypeStruct((m, c), F32),
                   jax.ShapeDtypeStruct((m, c), BF16), jax.ShapeDtypeStruct((m, c), BF16)],
        compiler_params=pltpu.CompilerParams(
            dimension_semantics=("parallel",), vmem_limit_bytes=_vmem_limit(blk)),
        name="rwkv_lora",
    )(lora_raw, lora_raw, mu.reshape(1, wl), w0.reshape(1, c), a0.reshape(1, c), w2, a2, g2)


def _bf(x):
    return x.astype(BF16)


def _dot(a, b):
    return jnp.dot(_bf(a), _bf(b), preferred_element_type=F32)


def _dot_nt(a, b):
    return lax.dot_general(_bf(a), _bf(b), (((1,), (1,)), ((), ())), preferred_element_type=F32)


def _dot_tn(a, b):
    return lax.dot_general(_bf(a), _bf(b), (((0,), (0,)), ((), ())), preferred_element_type=F32)


def _split_dot(x, ones_bf, parts, ones_left=False):
    acc, rem = None, x
    for p in range(parts):
        hi = _bf(rem)
        ops = (ones_bf, hi) if ones_left else (hi, ones_bf)
        term = jnp.dot(*ops, preferred_element_type=F32)
        acc = term if acc is None else acc + term
        if p + 1 < parts:
            rem = rem - hi.astype(F32)
    return acc


def _wkv_kernel(r_ref, k_ref, v_ref, cum_ref, a_ref, g_ref,
                mur_ref, muk_ref, muv_ref, kk_ref, ka_ref, rk_ref, lnw_ref, lnb_ref,
                o_ref, rm_scr, yg_scr, bonus_scr, y_scr, h_scr, *, n_chunks, group):
    c_len, two = CHUNK, 2 * CHUNK
    g_rows = group * c_len
    shift_c = CHUNK.bit_length() - 1
    lane = lax.broadcasted_iota(jnp.int32, (c_len, PAIR), 1)
    m0 = (lane < RWKV_HEAD).astype(F32)
    m1 = 1.0 - m0
    row = lax.broadcasted_iota(jnp.int32, (two, two), 0)
    col = lax.broadcasted_iota(jnp.int32, (two, two), 1)
    same = (row >> shift_c) == (col >> shift_c)
    strict = same & ((row & (CHUNK - 1)) > (col & (CHUNK - 1)))
    incl = same & ((row & (CHUNK - 1)) >= (col & (CHUNK - 1)))
    eye = (row == col).astype(F32)
    head_ones = _bf(same.astype(F32))
    g_row = lax.broadcasted_iota(jnp.int32, (g_rows, PAIR), 0)
    row0 = g_row == 0
    chunk_row0 = (g_row & (CHUNK - 1)) == 0

    mu_r, mu_k, mu_v = mur_ref[...], muk_ref[...], muv_ref[...]
    k_k, k_a, r_k = kk_ref[...], ka_ref[...], rk_ref[...]

    def stack(x):
        return jnp.concatenate([x * m0, x * m1], axis=0)

    rm_scr[pl.ds(0, group * 2 * two), :] = jnp.zeros((group * 2 * two, PAIR), BF16)
    yg_scr[pl.ds(0, group * 2 * two), :] = jnp.zeros((group * 2 * two, PAIR), F32)
    h_scr[...] = jnp.zeros_like(h_scr)
    y_scr[pl.ds(0, g_rows), :] = jnp.zeros((g_rows, PAIR), F32)

    def aligned(start, mult):
        return start if isinstance(start, int) else pl.multiple_of(start, mult)

    def serial_step(slot, h):
        base = aligned(slot * (2 * two), 2 * two)
        out = jnp.dot(rm_scr[pl.ds(base, 2 * two), :], h,
                      preferred_element_type=F32) + yg_scr[pl.ds(base, 2 * two), :]
        rows = pl.ds(aligned((slot + group) * c_len, c_len), c_len)
        y2 = out[two:]
        y_scr[rows, :] = y2[:c_len] + y2[c_len:]
        return _bf(out[:two])

    n_boundaries = 9
    serial_at = [j * n_boundaries // group for j in range(group)]
    n_groups = n_chunks // group
    ln_w, ln_b = lnw_ref[...], lnb_ref[...]
    inv_n = 1.0 / RWKV_HEAD

    def group_rows(gi):
        start = gi * g_rows
        return pl.ds(aligned(start, g_rows), g_rows)

    def centered(y_blocks):
        ys = [y_scr[group_rows(yb), :] for yb in y_blocks]
        return [y - _split_dot(y, head_ones, 1) * inv_n for y in ys]

    def normalised(ds, groups):
        rows = [group_rows(gi) for gi in groups]
        vs = [_split_dot(d * d, head_ones, 1) * inv_n for d in ds]
        return [((d * lax.rsqrt(var + GN_EPS) * ln_w + ln_b + bonus_scr[r, :])
                 * g_ref[r, :].astype(F32)).astype(o_ref.dtype) for r, d, var in zip(rows, ds, vs)]

    def finish(y_blocks, groups):
        return normalised(centered(y_blocks), groups)

    def local(i, _):
        rows = pl.ds(pl.multiple_of(i * g_rows, g_rows), g_rows)
        before = pl.ds(pl.multiple_of(jnp.maximum(i * g_rows - SUBLANES, 0), SUBLANES), SUBLANES)
        keep_prev = jnp.where(i > 0, 1.0, 0.0)
        state = [h_scr[...]]

        def boundary(k):
            for j in range(group):
                if serial_at[j] == k:
                    state[0] = serial_step(i * group + j, state[0])

        def shift(ref, mu):
            x = ref[rows, :]
            last = ref[before, :][SUBLANES - 1:, :] * keep_prev
            prev = jnp.where(row0, last, pltpu.roll(x, 1, axis=0))
            return x + (prev - x) * mu

        rs, ks, vs = shift(r_ref, mu_r), shift(k_ref, mu_k), shift(v_ref, mu_v)
        cum, a = cum_ref[rows, :], a_ref[rows, :].astype(F32)

        kk = ks * k_k
        norm = jnp.sqrt(_split_dot(kk * kk, head_ones, 1))
        kk = kk / jnp.maximum(norm, 1e-12)
        kx = ks * (1.0 + (a - 1.0) * k_a)
        b = kk * a
        bonus_scr[rows, :] = _split_dot(rs * kx * r_k, head_ones, 1) * vs
        (o_block,) = finish([i], [jnp.maximum(i - 2, 0)])

        cum_before = jnp.where(chunk_row0, 0.0, pltpu.roll(cum, 1, axis=0))
        e_neg = jnp.exp(-cum)
        at_g = -kk * jnp.exp(cum_before)
        rt_g = rs * jnp.exp(cum)
        bt_g, kt_g = _bf(b * e_neg), _bf(kx * e_neg)

        st = []
        for j in range(group):
            sl = slice(j * c_len, (j + 1) * c_len)
            c_end = cum[(j + 1) * c_len - 1:(j + 1) * c_len, :]
            e_rem = jnp.exp(c_end - cum[sl])
            at2, rt2 = stack(at_g[sl]), stack(rt_g[sl])
            st.append(dict(
                at2=at2, rt2=rt2, v2=_bf(stack(vs[sl])), d_end=jnp.exp(c_end),
                bk2=jnp.concatenate([stack(b[sl] * e_rem), stack(kx[sl] * e_rem)], axis=0),
                s=_dot_nt(jnp.concatenate([at2, rt2], axis=0),
                          jnp.concatenate([bt_g[sl], bt_g[sl], kt_g[sl], kt_g[sl]], axis=0))))
        boundary(0)
        for d in st:
            s = d.pop("s")
            x = jnp.where(strict, s[:two, :two], 0.0)
            a_ak = jnp.where(strict, s[:two, two:], 0.0)
            d["rbk"] = jnp.concatenate([jnp.where(incl, s[two:, :two], 0.0),
                                        jnp.where(incl, s[two:, two:], 0.0)], axis=1)
            d["t_sum"] = eye + x
            d["x_pow"] = _dot(x, x)
            d["z2"] = _dot(a_ak, d["v2"])
        boundary(1)
        for step in range(CHUNK.bit_length() - 3):
            for d in st:
                both = _dot(d["x_pow"], jnp.concatenate([d["t_sum"], d["x_pow"]], axis=1))
                d["t_sum"] = d["t_sum"] + both[:, :two]
                d["x_pow"] = both[:, two:]
            boundary(2 + step)
        for d in st:
            d["t_inv"] = d["t_sum"] + _dot(d["x_pow"], d["t_sum"])
        boundary(6)
        for d in st:
            pq = _dot(d["t_inv"], jnp.concatenate([_bf(d["at2"]), _bf(d["z2"])], axis=1))
            d["rhs"] = jnp.concatenate(
                [_bf(pq), jnp.concatenate([jnp.zeros_like(d["v2"]), d["v2"]], axis=1)], axis=0)
        boundary(7)
        for d in st:
            d["n1"] = _dot(d["rbk"], d["rhs"])
            d["n2"] = _dot_tn(d["bk2"], d["rhs"])
        boundary(8)
        for j, d in enumerate(st):
            n1, n2 = d["n1"], d["n2"]
            rp2 = d["rt2"] + n1[:, :two]
            m_blk = n2[:, :two] + eye * d["d_end"]
            base = pl.multiple_of(((i + 1) * group + j) * (2 * two), 2 * two)
            rm_scr[pl.ds(base, 2 * two), :] = _bf(jnp.concatenate([m_blk, rp2], axis=0))
            yg_scr[pl.ds(base, 2 * two), :] = jnp.concatenate([n2[:, two:], n1[:, two:]], axis=0)
        h_scr[...] = state[0]

        @pl.when(i >= 2)
        def _():
            o_ref[group_rows(jnp.maximum(i - 2, 0)), :] = o_block

        return 0

    lax.fori_loop(0, n_groups, local, 0)

    early = list(range(max(n_groups - 2, 0), n_groups - 1))
    h = h_scr[...]
    ds = centered([gi + 2 for gi in early])
    for j in range(group):
        h = serial_step(n_chunks + j, h)
        if j == 1:
            for gi, o_block in zip(early, normalised(ds, early)):
                o_ref[group_rows(gi), :] = o_block
    (o_block,) = finish([n_groups + 1], [n_groups - 1])
    o_ref[group_rows(n_groups - 1), :] = o_block


def _wkv(rkv, cum, a, g, mu_rkv, k_k, k_a, r_k, ln_w, ln_b, *, batch, seq, group=8):
    m, c = cum.shape
    n_pairs = c // PAIR
    n_chunks = seq // CHUNK
    assert seq % (group * CHUNK) == 0
    col = lambda off: pl.BlockSpec((seq, PAIR), lambda b, p: (b, p + off))
    vec = lambda off: pl.BlockSpec((1, PAIR), lambda b, p: (0, p + off))
    vecs = [x.reshape(1, c) for x in (k_k, k_a, r_k, ln_w, ln_b)]
    blk = 6 * _nbytes((seq, PAIR), F32) + _nbytes((seq, PAIR), BF16)
    slot_rows = (n_chunks + group) * 4 * CHUNK
    y_rows = seq + 2 * group * CHUNK
    scratch = [pltpu.VMEM((slot_rows, PAIR), BF16),
               pltpu.VMEM((slot_rows, PAIR), F32),
               pltpu.VMEM((seq, PAIR), F32),
               pltpu.VMEM((y_rows, PAIR), F32),
               pltpu.VMEM((PAIR, PAIR), BF16)]
    scr = (_nbytes((slot_rows, PAIR), BF16) + _nbytes((slot_rows, PAIR), F32)
           + _nbytes((seq, PAIR), F32) + _nbytes((y_rows, PAIR), F32))
    return pl.pallas_call(
        functools.partial(_wkv_kernel, n_chunks=n_chunks, group=group),
        grid=(batch, n_pairs),
        in_specs=[col(0), col(n_pairs), col(2 * n_pairs), col(0), col(0), col(0),
                  vec(0), vec(n_pairs), vec(2 * n_pairs)] + [vec(0)] * 5,
        out_specs=col(0),
        out_shape=jax.ShapeDtypeStruct((m, c), BF16),
        scratch_shapes=scratch,
        compiler_params=pltpu.CompilerParams(
            dimension_semantics=("parallel", "parallel"),
            vmem_limit_bytes=int(min(VMEM_CAP, 2 * blk + scr + 8 * 1024 * 1024))),
        name="wkv7_chunked",
    )(rkv, rkv, rkv, cum, a, g, mu_rkv, mu_rkv, mu_rkv, *vecs)


def _rope_tables(seq):
    pos = jnp.arange(seq, dtype=F32)
    inv_freq = ROPE_THETA ** (-jnp.arange(0, ATT_HEAD, 2, dtype=F32) / ATT_HEAD)
    ang = pos[:, None] * inv_freq[None, :]
    cos, sin = jnp.cos(ang), jnp.sin(ang)
    return jnp.concatenate([cos, cos], axis=-1), jnp.concatenate([-sin, sin], axis=-1)


def _pad_to(x, n, axis):
    pad = n - x.shape[axis]
    if pad == 0:
        return x
    widths = [(0, 0)] * x.ndim
    widths[axis] = (0, pad)
    return jnp.pad(x, widths)


def _round_up(n, mult):
    return -(-n // mult) * mult


def _layer(x2, batch, seq, norm_mix_pre, norm_mix_post, norm_ffn_pre, norm_ffn_post, w_in, b_qkv,
           att_sinks, mu_shift, w0, w2, a0, a2, g2, k_k, k_a, r_k, ln_x_w, ln_x_b,
           w_att_branch, w_rwkv_branch, w_out, w_ffn_gate, w_ffn_up, w_ffn_down):
    d = x2.shape[1]
    n_q = att_sinks.shape[0]
    q_w = n_q * ATT_HEAD
    qkv_w = b_qkv.shape[0]
    n_kv = (qkv_w - q_w) // (2 * ATT_HEAD)
    c = w0.shape[0]
    d_decay, d_aaa, d_gate = w2.shape[0], a2.shape[0], g2.shape[0]
    lora_w = d_decay + d_aaa + d_gate
    lora_pad = _round_up(lora_w, LANES)
    ffn = w_ffn_gate.shape[1]

    c0 = qkv_w
    lora_col0 = c0 + 3 * c
    in_dim = w_in.shape[1]
    main_rows = lora_col0 + lora_pad
    row_tile = max(t for t in range(16, 513, 16) if in_dim % t == 0 and main_rows % t == 0)
    (w_in_t,) = _cast_bf16(w_in.T, [(0, d, d)], rows_out=main_rows, tr=row_tile)
    w_gates_t = _bf(w_in.T[lora_col0 + lora_w:])
    mu_rkv = mu_shift[:3 * c].reshape(1, 3 * c)
    mu_lora = _pad_to(mu_shift[3 * c:], lora_pad, 0)
    g2_pad = _pad_to(_bf(g2), lora_pad - d_decay - d_aaa, 0)

    (h,) = _norm_call(x2, norm_mix_pre, emit_main=False)

    qkv = _matmul(h, w_in_t, w_t=True, n_out=c0, out_dtype=BF16, bias=b_qkv)
    rkv = _matmul(h, w_in_t, w_t=True, w_col0=c0, n_out=3 * c, out_dtype=F32)
    lora_raw = _matmul(h, w_in_t, w_t=True, w_col0=lora_col0, n_out=lora_pad, out_dtype=F32,
                       tn=lora_pad)
    gates = _matmul(h, w_gates_t, w_t=True, out_dtype=BF16, act="sigmoid")

    cos, sin = _rope_tables(seq)
    o_att = _attention(qkv, att_sinks.astype(F32), cos, sin, seq=seq, n_q=n_q, n_kv=n_kv)

    cum, a_gate, g_out = _lora(lora_raw, mu_lora, w0, a0, _bf(w2), _bf(a2), g2_pad,
                              seq=seq, d_decay=d_decay, d_aaa=d_aaa)
    o_rwkv = _wkv(rkv, cum, a_gate, g_out, mu_rkv, k_k, k_a, r_k.reshape(-1), ln_x_w, ln_x_b,
                  batch=batch, seq=seq)

    m_att = _matmul(o_att, _bf(w_att_branch), out_dtype=BF16, mul=gates, mul_col0=0)
    merged = _matmul(o_rwkv, _bf(w_rwkv_branch), out_dtype=BF16, mul=gates, mul_col0=d,
                     add=m_att)
    mixed = _matmul(merged, _bf(w_out), out_dtype=BF16)
    x1, h2 = _norm_call(mixed, norm_mix_post, res=x2, g_next=norm_ffn_pre)

    tn_ffn = 512
    ffn_pad = _round_up(ffn, 2 * tn_ffn)
    (w_gate,) = _cast_bf16(w_ffn_gate, [(0, ffn, ffn_pad)], tr=128)
    (w_up,) = _cast_bf16(w_ffn_up, [(0, ffn, ffn_pad)], tr=128)
    (w_dn,) = _cast_bf16(w_ffn_down, [(0, d, d)], rows_out=ffn_pad)
    act = _matmul(h2, w_gate, w_up=w_up, out_dtype=BF16, act="swiglu", tn=tn_ffn)
    f = _matmul(act, w_dn, out_dtype=BF16, tk=ffn_pad // 4)
    (out,) = _norm_call(f, norm_ffn_post, res=x1)
    return out


def kernel(x, norm_mix_pre, norm_mix_post, norm_ffn_pre, norm_ffn_post, w_in, b_qkv, att_sinks,
           mu_shift, w0, w2, a0, a2, g2, k_k, k_a, r_k, ln_x_w, ln_x_b,
           w_att_branch, w_rwkv_branch, w_out, w_ffn_gate, w_ffn_up, w_ffn_down):
    batch, seq, d = x.shape
    x2 = x.reshape(batch * seq, d)
    for l in range(w_in.shape[0]):
        x2 = _layer(x2, batch, seq, norm_mix_pre[l], norm_mix_post[l], norm_ffn_pre[l],
                    norm_ffn_post[l], w_in[l], b_qkv[l], att_sinks[l], mu_shift[l], w0[l], w2[l],
                    a0[l], a2[l], g2[l], k_k[l], k_a[l], r_k[l], ln_x_w[l], ln_x_b[l],
                    w_att_branch[l], w_rwkv_branch[l], w_out[l],
                    w_ffn_gate[l], w_ffn_up[l], w_ffn_down[l])
    return x2.reshape(batch, seq, d)
```

```python
import functools
import math

import jax
import jax.numpy as jnp
from jax import lax
from jax.experimental import pallas as pl
from jax.experimental.pallas import tpu as pltpu

F32 = jnp.float32
BF16 = jnp.bfloat16

LANES = 128
SUBLANES = 8
V7X_VMEM_BYTES = 64 * 1024 * 1024
VMEM_CAP = V7X_VMEM_BYTES - 6 * 1024 * 1024

ATT_HEAD = 128
ATT_GROUP = 4
WINDOW = 128
ATT_SUB = 2
ROPE_THETA = 10000.0
NEG_INF = -1e30
RWKV_HEAD = 64
CHUNK = 64
PAIR = 2 * RWKV_HEAD
GN_EPS = 64e-5
RMS_EPS = 1e-6
DECAY_SCALE = math.exp(-0.5)


def _vmem_limit(block_bytes):
    return int(min(VMEM_CAP, 2 * block_bytes + 16 * 1024 * 1024))


def _nbytes(shape, dtype):
    n = 1
    for s in shape:
        n *= s
    return n * jnp.dtype(dtype).itemsize


def _cast_kernel(x_ref, *o_refs, windows, n_row_blocks, pad_rows):
    i = pl.program_id(0)
    x = x_ref[...]
    for o_ref, (c0, width) in zip(o_refs, windows):
        o_ref[:, :width] = x[:, c0:c0 + width].astype(o_ref.dtype)
        if o_ref.shape[1] > width:
            o_ref[:, width:] = jnp.zeros((o_ref.shape[0], o_ref.shape[1] - width), o_ref.dtype)
        if pad_rows:
            @pl.when(i >= n_row_blocks)
            def _():
                o_ref[...] = jnp.zeros_like(o_ref)


def _cast_bf16(w, windows, *, rows_out=None, tr=256):
    rows, cols = w.shape
    rows_out = rows if rows_out is None else rows_out
    tr = min(tr, rows)
    assert rows % tr == 0 and rows_out % tr == 0
    n_in = rows // tr
    blk = _nbytes((tr, cols), F32) + sum(_nbytes((tr, wp), BF16) for _, _, wp in windows)
    return pl.pallas_call(
        functools.partial(_cast_kernel, windows=[(c0, wd) for c0, wd, _ in windows],
                          n_row_blocks=n_in, pad_rows=rows_out > rows),
        grid=(rows_out // tr,),
        in_specs=[pl.BlockSpec((tr, cols), lambda i: (jnp.minimum(i, n_in - 1), 0))],
        out_specs=[pl.BlockSpec((tr, wp), lambda i: (i, 0)) for _, _, wp in windows],
        out_shape=[jax.ShapeDtypeStruct((rows_out, wp), BF16) for _, _, wp in windows],
        compiler_params=pltpu.CompilerParams(
            dimension_semantics=("parallel",), vmem_limit_bytes=_vmem_limit(blk)),
        name="cast_bf16",
    )(w)


def _rms(x, g):
    return x * lax.rsqrt(jnp.mean(x * x, axis=-1, keepdims=True) + RMS_EPS) * g


def _norm_kernel(*refs, has_res, emit_main, second_norm):
    it = iter(refs)
    x_ref, g_ref = next(it), next(it)
    res_ref = next(it) if has_res else None
    g2_ref = next(it) if second_norm else None
    main_ref = next(it) if emit_main else None
    next_ref = next(it) if (second_norm or not emit_main) else None
    y = _rms(x_ref[...].astype(F32), g_ref[...])
    if has_res:
        y = res_ref[...] + y
    if emit_main:
        main_ref[...] = y
    if second_norm:
        next_ref[...] = _rms(y, g2_ref[...]).astype(next_ref.dtype)
    elif not emit_main:
        next_ref[...] = y.astype(next_ref.dtype)


def _norm_call(x, g, res=None, g_next=None, *, emit_main=True, tm=256):
    m, d = x.shape
    tm = min(tm, m)
    second_norm = g_next is not None
    assert emit_main or not second_norm
    row = pl.BlockSpec((tm, d), lambda i: (i, 0))
    vec = pl.BlockSpec((1, d), lambda i: (0, 0))
    args, specs = [x, g.reshape(1, d)], [row, vec]
    if res is not None:
        args.append(res); specs.append(row)
    if second_norm:
        args.append(g_next.reshape(1, d)); specs.append(vec)
    out_shape, out_specs = [], []
    if emit_main:
        out_shape.append(jax.ShapeDtypeStruct((m, d), F32)); out_specs.append(row)
    if second_norm or not emit_main:
        out_shape.append(jax.ShapeDtypeStruct((m, d), BF16)); out_specs.append(row)
    blk = _nbytes((tm, d), F32) * (len(args) + len(out_shape))
    outs = pl.pallas_call(
        functools.partial(_norm_kernel, has_res=res is not None, emit_main=emit_main,
                          second_norm=second_norm),
        grid=(m // tm,),
        in_specs=specs, out_specs=out_specs, out_shape=out_shape,
        compiler_params=pltpu.CompilerParams(
            dimension_semantics=("parallel",), vmem_limit_bytes=_vmem_limit(blk)),
        name="rmsnorm",
    )(*args)
    return outs


def _sigmoid(x):
    return 0.5 * jnp.tanh(0.5 * x) + 0.5


def _mm_kernel(*refs, nk, has_bias, act, has_mul, has_add, w_t):
    it = iter(refs)
    a_ref, w_ref = next(it), next(it)
    up_ref = next(it) if act == "swiglu" else None
    bias_ref = next(it) if has_bias else None
    mul_ref = next(it) if has_mul else None
    add_ref = next(it) if has_add else None
    o_ref = next(it)
    acc_ref = next(it) if nk > 1 else None

    def product():
        if w_t:
            return lax.dot_general(a_ref[...], w_ref[...], (((1,), (1,)), ((), ())),
                                   preferred_element_type=F32)
        return jnp.dot(a_ref[...], w_ref[...], preferred_element_type=F32)

    def finish(y):
        if has_bias:
            y = y + bias_ref[...]
        if act == "sigmoid":
            y = _sigmoid(y)
        elif act == "swiglu":
            up = jnp.dot(a_ref[...], up_ref[...], preferred_element_type=F32)
            y = y * _sigmoid(y) * up
        if has_mul:
            y = y * mul_ref[...].astype(F32)
        if has_add:
            y = y + add_ref[...].astype(F32)
        o_ref[...] = y.astype(o_ref.dtype)

    if nk == 1:
        finish(product())
    else:
        k = pl.program_id(2)

        @pl.when(k == 0)
        def _():
            acc_ref[...] = product()

        @pl.when(jnp.logical_and(k > 0, k < nk - 1))
        def _():
            acc_ref[...] += product()

        @pl.when(k == nk - 1)
        def _():
            finish(acc_ref[...] + product())


def _matmul(a, w, *, out_dtype, tm=1024, tn=1024, tk=None, bias=None, act=None, w_up=None,
            mul=None, mul_col0=0, add=None, w_col0=0, n_out=None, w_t=False):
    m, kdim = a.shape
    n_out = w.shape[0 if w_t else 1] if n_out is None else n_out
    assert not (w_t and w_up is not None)
    tm = min(tm, m)
    tk = kdim if tk is None else tk
    nk = kdim // tk
    assert m % tm == 0 and kdim % tk == 0
    assert n_out % tn == 0 and tn % LANES == 0 and w_col0 % tn == 0
    assert (act == "swiglu") == (w_up is not None) and (w_up is None or nk == 1)
    grid = (m // tm, n_out // tn, nk)
    w_off = w_col0 // tn

    args = [a, w]
    if w_t:
        w_spec = pl.BlockSpec((tn, tk), lambda i, j, k: (j + w_off, k))
    else:
        w_spec = pl.BlockSpec((tk, tn), lambda i, j, k: (k, j + w_off))
    specs = [pl.BlockSpec((tm, tk), lambda i, j, k: (i, k)), w_spec]
    blk = _nbytes((tm, tk), a.dtype) + _nbytes((tk, tn), w.dtype)
    if w_up is not None:
        args.append(w_up)
        specs.append(w_spec)
        blk += _nbytes((tk, tn), w_up.dtype)
    if bias is not None:
        args.append(bias.reshape(1, n_out).astype(F32))
        specs.append(pl.BlockSpec((1, tn), lambda i, j, k: (0, j)))
    if mul is not None:
        assert mul_col0 % tn == 0
        off = mul_col0 // tn
        args.append(mul)
        specs.append(pl.BlockSpec((tm, tn), lambda i, j, k: (i, j + off)))
        blk += _nbytes((tm, tn), mul.dtype)
    if add is not None:
        args.append(add)
        specs.append(pl.BlockSpec((tm, tn), lambda i, j, k: (i, j)))
        blk += _nbytes((tm, tn), add.dtype)
    blk += _nbytes((tm, tn), out_dtype)
    scratch = [pltpu.VMEM((tm, tn), F32)] if nk > 1 else []
    extra = _nbytes((tm, tn), F32) * (4 if (nk > 1 or w_up is not None) else 3)

    return pl.pallas_call(
        functools.partial(_mm_kernel, nk=nk, has_bias=bias is not None, act=act,
                          has_mul=mul is not None, has_add=add is not None, w_t=w_t),
        grid=grid,
        in_specs=specs,
        out_specs=pl.BlockSpec((tm, tn), lambda i, j, k: (i, j)),
        out_shape=jax.ShapeDtypeStruct((m, n_out), out_dtype),
        scratch_shapes=scratch,
        compiler_params=pltpu.CompilerParams(
            dimension_semantics=("parallel", "parallel", "arbitrary"),
            vmem_limit_bytes=int(min(VMEM_CAP, 2 * blk + extra + 4 * 1024 * 1024))),
        name="matmul_" + (act or "plain"),
    )(*args)


def _attn_kernel(sinks_ref, q_ref, kc_ref, kp_ref, vc_ref, vp_ref,
                 cosq_ref, sinq_ref, cosc_ref, sinc_ref, cosp_ref, sinp_ref, o_ref, bias_scr,
                 *, nb, n_kv):
    i = pl.program_id(0)
    blk = WINDOW
    first = ((ATT_SUB * i) % nb) == 0

    def rope(t, cos, sin):
        t = t.astype(F32)
        return t * cos + pltpu.roll(t, ATT_HEAD // 2, axis=1) * sin

    cos_q, sin_q = cosq_ref[...], sinq_ref[...]
    cos_c, sin_c = cosc_ref[...], sinc_ref[...]
    cos_p, sin_p = cosp_ref[...], sinp_ref[...]

    qi = lax.broadcasted_iota(jnp.int32, (blk, 2 * blk), 0)
    kj = lax.broadcasted_iota(jnp.int32, (blk, 2 * blk), 1)
    band = (kj > qi) & (kj <= qi + blk)
    lo = jnp.where(first, blk, 0)
    bias_scr[0] = jnp.where(band & (kj >= lo), 0.0, NEG_INF)
    bias_scr[1] = jnp.where(band, 0.0, NEG_INF)
    ones = jnp.ones((2 * blk, ATT_HEAD), BF16)

    def sub_rows(t):
        return slice(t * blk, (t + 1) * blk)

    def scores(h):
        cs = slice(h * ATT_HEAD, (h + 1) * ATT_HEAD)
        k_blocks = [rope(kp_ref[:, cs], cos_p, sin_p).astype(BF16)] + [
            rope(kc_ref[sub_rows(t), cs], cos_c[sub_rows(t)], sin_c[sub_rows(t)]).astype(BF16)
            for t in range(ATT_SUB)]
        out = []
        for t in range(ATT_SUB):
            k_band = jnp.concatenate([k_blocks[t], k_blocks[t + 1]], axis=0)
            q_all = jnp.concatenate(
                [rope(q_ref[sub_rows(t), hq * ATT_HEAD:(hq + 1) * ATT_HEAD],
                      cos_q[sub_rows(t)], sin_q[sub_rows(t)]).astype(BF16)
                 for hq in range(h * ATT_GROUP, (h + 1) * ATT_GROUP)], axis=0)
            out.append(lax.dot_general(q_all, k_band, (((1,), (1,)), ((), ())),
                                       preferred_element_type=F32))
        return out

    def finalize(h, pvs, sink_parts):
        for t in range(ATT_SUB):
            for g in range(ATT_GROUP):
                hq = h * ATT_GROUP + g
                pv_g = pvs[t][g * blk:(g + 1) * blk]
                o_g = pv_g[:, :ATT_HEAD] / (pv_g[:, ATT_HEAD:] + sink_parts[t][g])
                o_ref[sub_rows(t), hq * ATT_HEAD:(hq + 1) * ATT_HEAD] = o_g.astype(o_ref.dtype)

    s_next, pending = scores(0), None
    for h in range(n_kv):
        s = s_next
        if h + 1 < n_kv:
            s_next = scores(h + 1)
        cs = slice(h * ATT_HEAD, (h + 1) * ATT_HEAD)
        v_blocks = [vp_ref[:, cs]] + [vc_ref[sub_rows(t), cs] for t in range(ATT_SUB)]
        p_all, sink_parts = [], []
        for t in range(ATT_SUB):
            p_parts, sinks_t = [], []
            for g in range(ATT_GROUP):
                sink = sinks_ref[h * ATT_GROUP + g]
                s_g = s[t][g * blk:(g + 1) * blk] + bias_scr[t]
                m_g = jnp.maximum(jnp.max(s_g, axis=-1, keepdims=True), sink)
                p_parts.append(jnp.exp(s_g - m_g).astype(BF16))
                sinks_t.append(jnp.exp(sink - m_g))
            p_all.append(jnp.concatenate(p_parts, axis=0))
            sink_parts.append(sinks_t)
        pvs = []
        for t in range(ATT_SUB):
            v_ext = jnp.concatenate(
                [jnp.concatenate([v_blocks[t], v_blocks[t + 1]], axis=0), ones], axis=1)
            pvs.append(jnp.dot(p_all[t], v_ext, preferred_element_type=F32))
        if pending is not None:
            finalize(*pending)
        pending = (h, pvs, sink_parts)
    finalize(*pending)


def _attention(qkv, sinks, cos, sin, *, seq, n_q, n_kv):
    m = qkv.shape[0]
    nb = seq // WINDOW
    assert nb % ATT_SUB == 0
    step = ATT_SUB * WINDOW
    steps_per_seq = nb // ATT_SUB
    q_w, kv_w = n_q * ATT_HEAD, n_kv * ATT_HEAD
    k_blk0, v_blk0 = q_w // kv_w, q_w // kv_w + 1
    prev = lambda i: jnp.maximum(ATT_SUB * i - 1, 0)
    scale = ATT_HEAD ** -0.5
    tab_c = pl.BlockSpec((step, ATT_HEAD), lambda i: (i % steps_per_seq, 0))
    tab_p = pl.BlockSpec((WINDOW, ATT_HEAD),
                         lambda i: (jnp.maximum(ATT_SUB * (i % steps_per_seq) - 1, 0), 0))
    blk = (_nbytes((step, q_w), BF16) * 2 + 3 * _nbytes((step, kv_w), BF16)
           + 6 * _nbytes((step, ATT_HEAD), F32))
    return pl.pallas_call(
        functools.partial(_attn_kernel, nb=nb, n_kv=n_kv),
        grid=(m // step,),
        in_specs=[
            pl.BlockSpec(memory_space=pltpu.SMEM),
            pl.BlockSpec((step, q_w), lambda i: (i, 0)),
            pl.BlockSpec((step, kv_w), lambda i: (i, k_blk0)),
            pl.BlockSpec((WINDOW, kv_w), lambda i: (prev(i), k_blk0)),
            pl.BlockSpec((step, kv_w), lambda i: (i, v_blk0)),
            pl.BlockSpec((WINDOW, kv_w), lambda i: (prev(i), v_blk0)),
            tab_c, tab_c, tab_c, tab_c, tab_p, tab_p,
        ],
        out_specs=pl.BlockSpec((step, q_w), lambda i: (i, 0)),
        out_shape=jax.ShapeDtypeStruct((m, q_w), BF16),
        scratch_shapes=[pltpu.VMEM((ATT_SUB, WINDOW, 2 * WINDOW), F32)],
        compiler_params=pltpu.CompilerParams(
            dimension_semantics=("parallel",), vmem_limit_bytes=_vmem_limit(blk)),
        name="swa_attention",
    )(sinks, qkv, qkv, qkv, qkv, qkv, cos * scale, sin * scale, cos, sin, cos, sin)


def _lora_kernel(x_ref, prev_ref, mu_ref, w0_ref, a0_ref, w2_ref, a2_ref, g2_ref,
                 cum_ref, a_ref, g_ref, *, seq, d_decay, d_aaa):
    i = pl.program_id(0)
    tm = x_ref.shape[0]
    x = x_ref[...]
    last_prev = prev_ref[prev_ref.shape[0] - 1:, :]
    last_prev = jnp.where((i * tm) % seq == 0, 0.0, last_prev)
    row = lax.broadcasted_iota(jnp.int32, x.shape, 0)
    prev = jnp.where(row == 0, last_prev, pltpu.roll(x, 1, axis=0))
    xs = x + (prev - x) * mu_ref[...]
    p_w = jnp.tanh(xs[:, :d_decay]).astype(BF16)
    p_a = xs[:, d_decay:d_decay + d_aaa].astype(BF16)
    p_g = _sigmoid(xs[:, d_decay + d_aaa:]).astype(BF16)
    z = w0_ref[...] + jnp.dot(p_w, w2_ref[...], preferred_element_type=F32)
    log_decay = -DECAY_SCALE * _sigmoid(z)
    shift_c = CHUNK.bit_length() - 1
    tr = lax.broadcasted_iota(jnp.int32, (tm, tm), 0)
    tc = lax.broadcasted_iota(jnp.int32, (tm, tm), 1)
    tri_ones = (((tr >> shift_c) == (tc >> shift_c)) & (tr >= tc)).astype(F32).astype(BF16)
    cum_ref[...] = _split_dot(log_decay, tri_ones, 2, ones_left=True)
    a_ref[...] = _sigmoid(
        a0_ref[...] + jnp.dot(p_a, a2_ref[...], preferred_element_type=F32)).astype(a_ref.dtype)
    g_ref[...] = jnp.dot(p_g, g2_ref[...], preferred_element_type=F32).astype(g_ref.dtype)


def _lora(lora_raw, mu, w0, a0, w2, a2, g2, *, seq, d_decay, d_aaa, tm=256):
    m, wl = lora_raw.shape
    c = w2.shape[1]
    tm = min(tm, seq)
    assert seq % tm == 0 and tm % CHUNK == 0
    sub = SUBLANES
    vec = lambda n: pl.BlockSpec((1, n), lambda i: (0, 0))
    full = lambda a: pl.BlockSpec(a.shape, lambda i: (0, 0))
    out = pl.BlockSpec((tm, c), lambda i: (i, 0))
    blk = (_nbytes((tm, wl), F32) + 3 * _nbytes((tm, c), F32)
           + _nbytes(w2.shape, BF16) + _nbytes(a2.shape, BF16) + _nbytes(g2.shape, BF16))
    return pl.pallas_call(
        functools.partial(_lora_kernel, seq=seq, d_decay=d_decay, d_aaa=d_aaa),
        grid=(m // tm,),
        in_specs=[
            pl.BlockSpec((tm, wl), lambda i: (i, 0)),
            pl.BlockSpec((sub, wl), lambda i: (jnp.maximum(i * (tm // sub) - 1, 0), 0)),
            vec(wl), vec(c), vec(c), full(w2), full(a2), full(g2),
        ],
        out_specs=[out, out, out],
        out_shape=[jax.ShapeDtypeStruct((m, c), F32),
                   jax.ShapeDtypeStruct((m, c), BF16), jax.ShapeDtypeStruct((m, c), BF16)],
        compiler_params=pltpu.CompilerParams(
            dimension_semantics=("parallel",), vmem_limit_bytes=_vmem_limit(blk)),
        name="rwkv_lora",
    )(lora_raw, lora_raw, mu.reshape(1, wl), w0.reshape(1, c), a0.reshape(1, c), w2, a2, g2)


def _bf(x):
    return x.astype(BF16)


def _dot(a, b):
    return jnp.dot(_bf(a), _bf(b), preferred_element_type=F32)


def _dot_nt(a, b):
    return lax.dot_general(_bf(a), _bf(b), (((1,), (1,)), ((), ())), preferred_element_type=F32)


def _dot_tn(a, b):
    return lax.dot_general(_bf(a), _bf(b), (((0,), (0,)), ((), ())), preferred_element_type=F32)


def _split_dot(x, ones_bf, parts, ones_left=False):
    acc, rem = None, x
    for p in range(parts):
        hi = _bf(rem)
        ops = (ones_bf, hi) if ones_left else (hi, ones_bf)
        term = jnp.dot(*ops, preferred_element_type=F32)
        acc = term if acc is None else acc + term
        if p + 1 < parts:
            rem = rem - hi.astype(F32)
    return acc


def _wkv_kernel(r_ref, k_ref, v_ref, cum_ref, a_ref, g_ref,
                mur_ref, muk_ref, muv_ref, kk_ref, ka_ref, rk_ref, lnw_ref, lnb_ref,
                o_ref, rm_scr, yg_scr, bonus_scr, y_scr, h_scr, *, n_chunks, group):
    c_len, two = CHUNK, 2 * CHUNK
    g_rows = group * c_len
    shift_c = CHUNK.bit_length() - 1
    lane = lax.broadcasted_iota(jnp.int32, (c_len, PAIR), 1)
    m0 = (lane < RWKV_HEAD).astype(F32)
    m1 = 1.0 - m0
    row = lax.broadcasted_iota(jnp.int32, (two, two), 0)
    col = lax.broadcasted_iota(jnp.int32, (two, two), 1)
    same = (row >> shift_c) == (col >> shift_c)
    strict = same & ((row & (CHUNK - 1)) > (col & (CHUNK - 1)))
    incl = same & ((row & (CHUNK - 1)) >= (col & (CHUNK - 1)))
    eye = (row == col).astype(F32)
    head_ones = _bf(same.astype(F32))
    g_row = lax.broadcasted_iota(jnp.int32, (g_rows, PAIR), 0)
    row0 = g_row == 0
    chunk_row0 = (g_row & (CHUNK - 1)) == 0

    mu_r, mu_k, mu_v = mur_ref[...], muk_ref[...], muv_ref[...]
    k_k, k_a, r_k = kk_ref[...], ka_ref[...], rk_ref[...]

    def stack(x):
        return jnp.concatenate([x * m0, x * m1], axis=0)

    rm_scr[pl.ds(0, group * 2 * two), :] = jnp.zeros((group * 2 * two, PAIR), BF16)
    yg_scr[pl.ds(0, group * 2 * two), :] = jnp.zeros((group * 2 * two, PAIR), F32)
    h_scr[...] = jnp.zeros_like(h_scr)
    y_scr[pl.ds(0, g_rows), :] = jnp.zeros((g_rows, PAIR), F32)

    def aligned(start, mult):
        return start if isinstance(start, int) else pl.multiple_of(start, mult)

    def serial_step(slot, h):
        base = aligned(slot * (2 * two), 2 * two)
        out = jnp.dot(rm_scr[pl.ds(base, 2 * two), :], h,
                      preferred_element_type=F32) + yg_scr[pl.ds(base, 2 * two), :]
        rows = pl.ds(aligned((slot + group) * c_len, c_len), c_len)
        y2 = out[two:]
        y_scr[rows, :] = y2[:c_len] + y2[c_len:]
        return _bf(out[:two])

    n_boundaries = 9
    serial_at = [j * n_boundaries // group for j in range(group)]
    n_groups = n_chunks // group
    ln_w, ln_b = lnw_ref[...], lnb_ref[...]
    inv_n = 1.0 / RWKV_HEAD

    def group_rows(gi):
        start = gi * g_rows
        return pl.ds(aligned(start, g_rows), g_rows)

    def centered(y_blocks):
        ys = [y_scr[group_rows(yb), :] for yb in y_blocks]
        return [y - _split_dot(y, head_ones, 1) * inv_n for y in ys]

    def normalised(ds, groups):
        rows = [group_rows(gi) for gi in groups]
        vs = [_split_dot(d * d, head_ones, 1) * inv_n for d in ds]
        return [((d * lax.rsqrt(var + GN_EPS) * ln_w + ln_b + bonus_scr[r, :])
                 * g_ref[r, :].astype(F32)).astype(o_ref.dtype) for r, d, var in zip(rows, ds, vs)]

    def finish(y_blocks, groups):
        return normalised(centered(y_blocks), groups)

    def local(i, _):
        rows = pl.ds(pl.multiple_of(i * g_rows, g_rows), g_rows)
        before = pl.ds(pl.multiple_of(jnp.maximum(i * g_rows - SUBLANES, 0), SUBLANES), SUBLANES)
        keep_prev = jnp.where(i > 0, 1.0, 0.0)
        state = [h_scr[...]]

        def boundary(k):
            for j in range(group):
                if serial_at[j] == k:
                    state[0] = serial_step(i * group + j, state[0])

        def shift(ref, mu):
            x = ref[rows, :]
            last = ref[before, :][SUBLANES - 1:, :] * keep_prev
            prev = jnp.where(row0, last, pltpu.roll(x, 1, axis=0))
            return x + (prev - x) * mu

        rs, ks, vs = shift(r_ref, mu_r), shift(k_ref, mu_k), shift(v_ref, mu_v)
        cum, a = cum_ref[rows, :], a_ref[rows, :].astype(F32)

        kk = ks * k_k
        norm = jnp.sqrt(_split_dot(kk * kk, head_ones, 1))
        kk = kk / jnp.maximum(norm, 1e-12)
        kx = ks * (1.0 + (a - 1.0) * k_a)
        b = kk * a
        bonus_scr[rows, :] = _split_dot(rs * kx * r_k, head_ones, 1) * vs
        (o_block,) = finish([i], [jnp.maximum(i - 2, 0)])

        cum_before = jnp.where(chunk_row0, 0.0, pltpu.roll(cum, 1, axis=0))
        e_neg = jnp.exp(-cum)
        at_g = -kk * jnp.exp(cum_before)
        rt_g = rs * jnp.exp(cum)
        bt_g, kt_g = _bf(b * e_neg), _bf(kx * e_neg)

        st = []
        for j in range(group):
            sl = slice(j * c_len, (j + 1) * c_len)
            c_end = cum[(j + 1) * c_len - 1:(j + 1) * c_len, :]
            e_rem = jnp.exp(c_end - cum[sl])
            at2, rt2 = stack(at_g[sl]), stack(rt_g[sl])
            st.append(dict(
                at2=at2, rt2=rt2, v2=_bf(stack(vs[sl])), d_end=jnp.exp(c_end),
                bk2=jnp.concatenate([stack(b[sl] * e_rem), stack(kx[sl] * e_rem)], axis=0),
                s=_dot_nt(jnp.concatenate([at2, rt2], axis=0),
                          jnp.concatenate([bt_g[sl], bt_g[sl], kt_g[sl], kt_g[sl]], axis=0))))
        boundary(0)
        for d in st:
            s = d.pop("s")
            x = jnp.where(strict, s[:two, :two], 0.0)
            a_ak = jnp.where(strict, s[:two, two:], 0.0)
            d["rbk"] = jnp.concatenate([jnp.where(incl, s[two:, :two], 0.0),
                                        jnp.where(incl, s[two:, two:], 0.0)], axis=1)
            d["t_sum"] = eye + x
            d["x_pow"] = _dot(x, x)
            d["z2"] = _dot(a_ak, d["v2"])
        boundary(1)
        for step in range(CHUNK.bit_length() - 3):
            for d in st:
                both = _dot(d["x_pow"], jnp.concatenate([d["t_sum"], d["x_pow"]], axis=1))
                d["t_sum"] = d["t_sum"] + both[:, :two]
                d["x_pow"] = both[:, two:]
            boundary(2 + step)
        for d in st:
            d["t_inv"] = d["t_sum"] + _dot(d["x_pow"], d["t_sum"])
        boundary(6)
        for d in st:
            pq = _dot(d["t_inv"], jnp.concatenate([_bf(d["at2"]), _bf(d["z2"])], axis=1))
            d["rhs"] = jnp.concatenate(
                [_bf(pq), jnp.concatenate([jnp.zeros_like(d["v2"]), d["v2"]], axis=1)], axis=0)
        boundary(7)
        for d in st:
            d["n1"] = _dot(d["rbk"], d["rhs"])
            d["n2"] = _dot_tn(d["bk2"], d["rhs"])
        boundary(8)
        for j, d in enumerate(st):
            n1, n2 = d["n1"], d["n2"]
            rp2 = d["rt2"] + n1[:, :two]
            m_blk = n2[:, :two] + eye * d["d_end"]
            base = pl.multiple_of(((i + 1) * group + j) * (2 * two), 2 * two)
            rm_scr[pl.ds(base, 2 * two), :] = _bf(jnp.concatenate([m_blk, rp2], axis=0))
            yg_scr[pl.ds(base, 2 * two), :] = jnp.concatenate([n2[:, two:], n1[:, two:]], axis=0)
        h_scr[...] = state[0]

        @pl.when(i >= 2)
        def _():
            o_ref[group_rows(jnp.maximum(i - 2, 0)), :] = o_block

        return 0

    lax.fori_loop(0, n_groups, local, 0)

    early = list(range(max(n_groups - 2, 0), n_groups - 1))
    h = h_scr[...]
    ds = centered([gi + 2 for gi in early])
    for j in range(group):
        h = serial_step(n_chunks + j, h)
        if j == 1:
            for gi, o_block in zip(early, normalised(ds, early)):
                o_ref[group_rows(gi), :] = o_block
    (o_block,) = finish([n_groups + 1], [n_groups - 1])
    o_ref[group_rows(n_groups - 1), :] = o_block


def _wkv(rkv, cum, a, g, mu_rkv, k_k, k_a, r_k, ln_w, ln_b, *, batch, seq, group=8):
    m, c = cum.shape
    n_pairs = c // PAIR
    n_chunks = seq // CHUNK
    assert seq % (group * CHUNK) == 0
    col = lambda off: pl.BlockSpec((seq, PAIR), lambda b, p: (b, p + off))
    vec = lambda off: pl.BlockSpec((1, PAIR), lambda b, p: (0, p + off))
    vecs = [x.reshape(1, c) for x in (k_k, k_a, r_k, ln_w, ln_b)]
    blk = 6 * _nbytes((seq, PAIR), F32) + _nbytes((seq, PAIR), BF16)
    slot_rows = (n_chunks + group) * 4 * CHUNK
    y_rows = seq + 2 * group * CHUNK
    scratch = [pltpu.VMEM((slot_rows, PAIR), BF16),
               pltpu.VMEM((slot_rows, PAIR), F32),
               pltpu.VMEM((seq, PAIR), F32),
               pltpu.VMEM((y_rows, PAIR), F32),
               pltpu.VMEM((PAIR, PAIR), BF16)]
    scr = (_nbytes((slot_rows, PAIR), BF16) + _nbytes((slot_rows, PAIR), F32)
           + _nbytes((seq, PAIR), F32) + _nbytes((y_rows, PAIR), F32))
    return pl.pallas_call(
        functools.partial(_wkv_kernel, n_chunks=n_chunks, group=group),
        grid=(batch, n_pairs),
        in_specs=[col(0), col(n_pairs), col(2 * n_pairs), col(0), col(0), col(0),
                  vec(0), vec(n_pairs), vec(2 * n_pairs)] + [vec(0)] * 5,
        out_specs=col(0),
        out_shape=jax.ShapeDtypeStruct((m, c), BF16),
        scratch_shapes=scratch,
        compiler_params=pltpu.CompilerParams(
            dimension_semantics=("parallel", "parallel"),
            vmem_limit_bytes=int(min(VMEM_CAP, 2 * blk + scr + 8 * 1024 * 1024))),
        name="wkv7_chunked",
    )(rkv, rkv, rkv, cum, a, g, mu_rkv, mu_rkv, mu_rkv, *vecs)


def _rope_tables(seq):
    pos = jnp.arange(seq, dtype=F32)
    inv_freq = ROPE_THETA ** (-jnp.arange(0, ATT_HEAD, 2, dtype=F32) / ATT_HEAD)
    ang = pos[:, None] * inv_freq[None, :]
    cos, sin = jnp.cos(ang), jnp.sin(ang)
    return jnp.concatenate([cos, cos], axis=-1), jnp.concatenate([-sin, sin], axis=-1)


def _pad_to(x, n, axis):
    pad = n - x.shape[axis]
    if pad == 0:
        return x
    widths = [(0, 0)] * x.ndim
    widths[axis] = (0, pad)
    return jnp.pad(x, widths)


def _round_up(n, mult):
    return -(-n // mult) * mult


def _layer(x2, batch, seq, norm_mix_pre, norm_mix_post, norm_ffn_pre, norm_ffn_post, w_in, b_qkv,
           att_sinks, mu_shift, w0, w2, a0, a2, g2, k_k, k_a, r_k, ln_x_w, ln_x_b,
           w_att_branch, w_rwkv_branch, w_out, w_ffn_gate, w_ffn_up, w_ffn_down):
    d = x2.shape[1]
    n_q = att_sinks.shape[0]
    q_w = n_q * ATT_HEAD
    qkv_w = b_qkv.shape[0]
    n_kv = (qkv_w - q_w) // (2 * ATT_HEAD)
    c = w0.shape[0]
    d_decay, d_aaa, d_gate = w2.shape[0], a2.shape[0], g2.shape[0]
    lora_w = d_decay + d_aaa + d_gate
    lora_pad = _round_up(lora_w, LANES)
    ffn = w_ffn_gate.shape[1]

    c0 = qkv_w
    lora_col0 = c0 + 3 * c
    in_dim = w_in.shape[1]
    main_rows = lora_col0 + lora_pad
    row_tile = max(t for t in range(16, 513, 16) if in_dim % t == 0 and main_rows % t == 0)
    (w_in_t,) = _cast_bf16(w_in.T, [(0, d, d)], rows_out=main_rows, tr=row_tile)
    w_gates_t = _bf(w_in.T[lora_col0 + lora_w:])
    mu_rkv = mu_shift[:3 * c].reshape(1, 3 * c)
    mu_lora = _pad_to(mu_shift[3 * c:], lora_pad, 0)
    g2_pad = _pad_to(_bf(g2), lora_pad - d_decay - d_aaa, 0)

    (h,) = _norm_call(x2, norm_mix_pre, emit_main=False)

    qkv = _matmul(h, w_in_t, w_t=True, n_out=c0, out_dtype=BF16, bias=b_qkv)
    rkv = _matmul(h, w_in_t, w_t=True, w_col0=c0, n_out=3 * c, out_dtype=F32)
    lora_raw = _matmul(h, w_in_t, w_t=True, w_col0=lora_col0, n_out=lora_pad, out_dtype=F32,
                       tn=lora_pad)
    gates = _matmul(h, w_gates_t, w_t=True, out_dtype=BF16, act="sigmoid")

    cos, sin = _rope_tables(seq)
    o_att = _attention(qkv, att_sinks.astype(F32), cos, sin, seq=seq, n_q=n_q, n_kv=n_kv)

    cum, a_gate, g_out = _lora(lora_raw, mu_lora, w0, a0, _bf(w2), _bf(a2), g2_pad,
                              seq=seq, d_decay=d_decay, d_aaa=d_aaa)
    o_rwkv = _wkv(rkv, cum, a_gate, g_out, mu_rkv, k_k, k_a, r_k.reshape(-1), ln_x_w, ln_x_b,
                  batch=batch, seq=seq)

    m_att = _matmul(o_att, _bf(w_att_branch), out_dtype=BF16, mul=gates, mul_col0=0)
    merged = _matmul(o_rwkv, _bf(w_rwkv_branch), out_dtype=BF16, mul=gates, mul_col0=d,
                     add=m_att)
    mixed = _matmul(merged, _bf(w_out), out_dtype=BF16)
    x1, h2 = _norm_call(mixed, norm_mix_post, res=x2, g_next=norm_ffn_pre)

    tn_ffn = 512
    ffn_pad = _round_up(ffn, 2 * tn_ffn)
    (w_gate,) = _cast_bf16(w_ffn_gate, [(0, ffn, ffn_pad)], tr=128)
    (w_up,) = _cast_bf16(w_ffn_up, [(0, ffn, ffn_pad)], tr=128)
    (w_dn,) = _cast_bf16(w_ffn_down, [(0, d, d)], rows_out=ffn_pad)
    act = _matmul(h2, w_gate, w_up=w_up, out_dtype=BF16, act="swiglu", tn=tn_ffn)
    f = _matmul(act, w_dn, out_dtype=BF16, tk=ffn_pad // 4)
    (out,) = _norm_call(f, norm_ffn_post, res=x1)
    return out


def kernel(x, norm_mix_pre, norm_mix_post, norm_ffn_pre, norm_ffn_post, w_in, b_qkv, att_sinks,
           mu_shift, w0, w2, a0, a2, g2, k_k, k_a, r_k, ln_x_w, ln_x_b,
           w_att_branch, w_rwkv_branch, w_out, w_ffn_gate, w_ffn_up, w_ffn_down):
    batch, seq, d = x.shape
    x2 = x.reshape(batch * seq, d)
    for l in range(w_in.shape[0]):
        x2 = _layer(x2, batch, seq, norm_mix_pre[l], norm_mix_post[l], norm_ffn_pre[l],
                    norm_ffn_post[l], w_in[l], b_qkv[l], att_sinks[l], mu_shift[l], w0[l], w2[l],
                    a0[l], a2[l], g2[l], k_k[l], k_a[l], r_k[l], ln_x_w[l], ln_x_b[l],
                    w_att_branch[l], w_rwkv_branch[l], w_out[l],
                    w_ffn_gate[l], w_ffn_up[l], w_ffn_down[l])
    return x2.reshape(batch, seq, d)
```
